```python
import math
import jax, jax.numpy as jnp
from jax import lax
import numpy as np

D_MODEL = 2048
BATCH = 8
SEQ = 2048
DEPTH = 4

N_MIXERS = 3
N_A = (DEPTH + 2) // 3
N_B = (DEPTH + 1) // 3
N_C = DEPTH // 3

MIX_WIDTH = 3 * D_MODEL // 4
MEM_LEN = 256
MEM_HEADS = 4
MEM_HEAD_DIM = D_MODEL // 16
MEM_WIDTH = MEM_HEADS * MEM_HEAD_DIM
OUT_WIDTH = MIX_WIDTH + MEM_WIDTH
NORM_EPS = 1e-6

SWA_HEAD_DIM = 64
SWA_Q_HEADS = MIX_WIDTH // SWA_HEAD_DIM
SWA_KV_HEADS = 4
SWA_GROUP = SWA_Q_HEADS // SWA_KV_HEADS
SWA_WINDOW = 128
SWA_BLOCK = 128
A_WIDTHS = (SWA_Q_HEADS * SWA_HEAD_DIM, SWA_KV_HEADS * SWA_HEAD_DIM, SWA_KV_HEADS * SWA_HEAD_DIM, MEM_WIDTH)

RWKV_HEAD_DIM = 64
RWKV_HEADS = MIX_WIDTH // RWKV_HEAD_DIM
RWKV_DECAY_RANK = 96
RWKV_ICLR_RANK = 96
RWKV_GATE_RANK = 256
RWKV_GN_EPS = 64e-5
B_SHIFT_WIDTHS = (MIX_WIDTH, MIX_WIDTH, MIX_WIDTH, RWKV_DECAY_RANK, RWKV_ICLR_RANK, RWKV_GATE_RANK)
B_SHIFT = sum(B_SHIFT_WIDTHS)

GDN_HEAD_DIM = 128
GDN_V_HEADS = MIX_WIDTH // GDN_HEAD_DIM
GDN_QK_HEADS = GDN_V_HEADS // 2
GDN_CONV = 4
GDN_CHUNK = 64
GDN_QK_WIDTH = GDN_QK_HEADS * GDN_HEAD_DIM
GDN_CONV_WIDTH = 2 * GDN_QK_WIDTH + MIX_WIDTH
C_WIDTHS = (GDN_QK_WIDTH, GDN_QK_WIDTH, MIX_WIDTH, MIX_WIDTH, GDN_V_HEADS, GDN_V_HEADS, MEM_WIDTH)

D_FF = 5632
FFN_CONV = 3

kernel_name = "hybrid_swa_rwkv7_gdn_memxattn_convffn"


def split_cols(p, widths):
    return jnp.split(p, [int(i) for i in np.cumsum(widths)[:-1]], axis=-1)


def rmsnorm(x, g):
    xf = x.astype(jnp.float32)
    y = xf * lax.rsqrt(jnp.mean(xf * xf, axis=-1, keepdims=True) + NORM_EPS)
    return (y * g.astype(jnp.float32)).astype(x.dtype)


def l2norm(x):
    x = x.astype(jnp.float32)
    return x * lax.rsqrt(jnp.sum(x * x, axis=-1, keepdims=True) + 1e-6)


def token_shift(x):
    return jnp.pad(x, ((0, 0), (1, 0), (0, 0)))[:, :-1]


def causal_dwconv(x, w):
    k_w, s = w.shape[0], x.shape[1]
    xp = jnp.pad(x, ((0, 0), (k_w - 1, 0), (0, 0)))
    out = xp[:, :s] * w[0]
    for j in range(1, k_w):
        out = out + xp[:, j:j + s] * w[j]
    return out


def alibi_slopes(n):
    return jnp.exp2(-8.0 * (jnp.arange(n, dtype=jnp.float32) + 1.0) / n)


def swa_sink_attention(q, k, v, sinks):
    b, s, _ = q.shape
    t, nb, dh = SWA_BLOCK, s // SWA_BLOCK, SWA_HEAD_DIM
    qb = q.reshape(b, nb, t, SWA_KV_HEADS, SWA_GROUP, dh)

    def banded(z):
        zb = z.reshape(b, nb, t, SWA_KV_HEADS, dh)
        prev = jnp.pad(zb, ((0, 0), (1, 0), (0, 0), (0, 0), (0, 0)))[:, :-1]
        return jnp.concatenate([prev, zb], axis=2)

    kb, vb = banded(k), banded(v)
    scores = jnp.einsum('bntkgd,bnjkd->bnkgtj', qb, kb).astype(jnp.float32) * (dh ** -0.5)
    blk = jnp.arange(nb)[:, None]
    qpos = blk * t + jnp.arange(t)[None, :]
    kpos = (blk - 1) * t + jnp.arange(2 * t)[None, :]
    dist = qpos[:, :, None] - kpos[:, None, :]
    valid = (dist >= 0) & (dist < SWA_WINDOW) & (kpos[:, None, :] >= 0)
    slopes = alibi_slopes(SWA_Q_HEADS).reshape(SWA_KV_HEADS, SWA_GROUP)
    bias = -slopes[None, :, :, None, None] * dist[:, None, None].astype(jnp.float32)
    scores = jnp.where(valid[:, None, None], scores + bias, -jnp.inf)
    sink = jnp.broadcast_to(sinks.astype(jnp.float32).reshape(1, 1, SWA_KV_HEADS, SWA_GROUP, 1, 1),
                            scores.shape[:-1] + (1,))
    probs = jax.nn.softmax(jnp.concatenate([scores, sink], axis=-1), axis=-1)[..., :-1]
    out = jnp.einsum('bnkgtj,bnjkd->bntkgd', probs.astype(v.dtype), vb)
    return out.reshape(b, s, SWA_Q_HEADS * dh)


def memory_attention(q, mem_kv):
    b, s, _ = q.shape
    qh = q.reshape(b, s, MEM_HEADS, MEM_HEAD_DIM)
    k, v = jnp.split(mem_kv, 2, axis=-1)
    kh = k.reshape(b, -1, MEM_HEADS, MEM_HEAD_DIM)
    vh = v.reshape(b, -1, MEM_HEADS, MEM_HEAD_DIM)
    scores = jnp.einsum('bshd,bmhd->bhsm', qh, kh).astype(jnp.float32) * (MEM_HEAD_DIM ** -0.5)
    probs = jax.nn.softmax(scores, axis=-1).astype(vh.dtype)
    return jnp.einsum('bhsm,bmhd->bshd', probs, vh).reshape(b, s, MEM_WIDTH)


def rwkv7_scan(r, w, k, v, kk, a):
    b, s, h, n = r.shape

    def step(state, inp):
        r_t, w_t, k_t, v_t, kk_t, a_t = inp
        sa = jnp.einsum('bhvk,bhk->bhv', state, -kk_t)
        state = (state * w_t[:, :, None, :] + sa[..., None] * (kk_t * a_t)[:, :, None, :]
                 + v_t[..., None] * k_t[:, :, None, :])
        return state, jnp.einsum('bhvk,bhk->bhv', state, r_t)

    xs = tuple(jnp.moveaxis(z, 1, 0) for z in (r, w, k, v, kk, a))
    _, ys = lax.scan(step, jnp.zeros((b, h, n, n), jnp.float32), xs)
    return jnp.moveaxis(ys, 0, 1)


def rwkv7_time_mix(p, mu, w0, w_decay_up, a0, w_iclr_up, w_gate_up, k_k, k_a, r_k, gn_g, gn_b):
    b, s, _ = p.shape
    f32 = jnp.float32
    h, n = RWKV_HEADS, RWKV_HEAD_DIM
    p = p + (token_shift(p) - p) * mu
    r, k, v, wd, ad, gd = split_cols(p, B_SHIFT_WIDTHS)
    w_log = -jax.nn.softplus(-(w0 + jnp.tanh(wd) @ w_decay_up).astype(f32)) - 0.5
    decay = jnp.exp(-jnp.exp(w_log))
    a = jax.nn.sigmoid((a0 + ad @ w_iclr_up).astype(f32))
    g = (jax.nn.sigmoid(gd) @ w_gate_up).astype(f32)
    k = k.astype(f32)
    heads = lambda z: z.reshape(b, s, h, n)
    kk = l2norm(heads(k * k_k))
    k = k * (1.0 + (a - 1.0) * k_a)
    rh, kh, vh = heads(r.astype(f32)), heads(k), heads(v.astype(f32))
    y = rwkv7_scan(rh, heads(decay), kh, vh, kk, heads(a))
    mean = jnp.mean(y, axis=-1, keepdims=True)
    var = jnp.mean(jnp.square(y - mean), axis=-1, keepdims=True)
    y = ((y - mean) * lax.rsqrt(var + RWKV_GN_EPS)).reshape(b, s, MIX_WIDTH) * gn_g + gn_b
    bonus = jnp.sum(rh * kh * r_k, axis=-1, keepdims=True) * vh
    y = y + bonus.reshape(b, s, MIX_WIDTH)
    return (y * g).astype(p.dtype)


def chunk_gated_delta_rule(q, k, v, g, beta):
    b, s, h, dk = q.shape
    dv = v.shape[-1]
    c = GDN_CHUNK
    nc = s // c
    chunks = lambda z: jnp.moveaxis(z.reshape(b, nc, c, h, -1), 3, 1)
    q = chunks(q * (dk ** -0.5))
    k = chunks(k)
    v = chunks(v)
    beta = chunks(beta[..., None])
    gc = jnp.cumsum(chunks(g[..., None])[..., 0], axis=-1)
    idx = jnp.arange(c)
    causal = idx[:, None] >= idx[None, :]
    strict = idx[:, None] > idx[None, :]
    decay = jnp.exp(jnp.where(causal, gc[..., :, None] - gc[..., None, :], -jnp.inf))
    kb = k * beta
    lmat = jnp.where(strict, jnp.einsum('bhnid,bhnjd->bhnij', kb, k) * decay, 0.0)
    eye = jnp.eye(c, dtype=lmat.dtype)
    tmat = lax.linalg.triangular_solve(lmat + eye, jnp.broadcast_to(eye, lmat.shape),
                                       left_side=True, lower=True, unit_diagonal=True)
    u = tmat @ (v * beta)
    w = tmat @ (kb * jnp.exp(gc)[..., None])
    a_qk = jnp.where(causal, jnp.einsum('bhnid,bhnjd->bhnij', q, k) * decay, 0.0)
    q_dec = q * jnp.exp(gc)[..., None]
    g_last = gc[..., -1]
    k_dec = k * jnp.exp(g_last[..., None] - gc)[..., None]

    def step(state, inp):
        u_c, w_c, qd_c, a_c, kd_c, gl_c = inp
        v_new = u_c - w_c @ state
        out = qd_c @ state + a_c @ v_new
        state = state * jnp.exp(gl_c)[..., None, None] + jnp.swapaxes(kd_c, -1, -2) @ v_new
        return state, out

    xs = tuple(jnp.moveaxis(z, 2, 0) for z in (u, w, q_dec, a_qk, k_dec, g_last))
    _, out = lax.scan(step, jnp.zeros((b, h, dk, dv), jnp.float32), xs)
    return jnp.transpose(out, (1, 0, 3, 2, 4)).reshape(b, s, h, dv)


def gated_deltanet(p, conv_w, a_log, dt_bias, norm_g):
    b, s, _ = p.shape
    f32 = jnp.float32
    qkv, z, bt, at = split_cols(p, (GDN_CONV_WIDTH, MIX_WIDTH, GDN_V_HEADS, GDN_V_HEADS))
    qkv = jax.nn.silu(causal_dwconv(qkv, conv_w))
    q, k, v = split_cols(qkv, (GDN_QK_WIDTH, GDN_QK_WIDTH, MIX_WIDTH))
    rep = GDN_V_HEADS // GDN_QK_HEADS
    q = jnp.repeat(l2norm(q.reshape(b, s, GDN_QK_HEADS, GDN_HEAD_DIM)), rep, axis=2)
    k = jnp.repeat(l2norm(k.reshape(b, s, GDN_QK_HEADS, GDN_HEAD_DIM)), rep, axis=2)
    v = v.reshape(b, s, GDN_V_HEADS, GDN_HEAD_DIM).astype(f32)
    beta = jax.nn.sigmoid(bt.astype(f32))
    g = -jnp.exp(a_log.astype(f32)) * jax.nn.softplus(at.astype(f32) + dt_bias.astype(f32))
    o = chunk_gated_delta_rule(q, k, v, g, beta)
    o = o * lax.rsqrt(jnp.mean(o * o, axis=-1, keepdims=True) + NORM_EPS) * norm_g.astype(f32)
    o = o.reshape(b, s, MIX_WIDTH) * jax.nn.silu(z.astype(f32))
    return o.astype(p.dtype)


def setup_inputs(seed: int = 0) -> dict:
    key = jax.random.key(seed)
    ks = iter(jax.random.split(key, 48))
    nrm = lambda shape, scale: jax.random.normal(next(ks), shape, jnp.float32) * scale
    gain = lambda shape: 1.0 + nrm(shape, 0.02)
    unif = lambda shape, lo, hi: jax.random.uniform(next(ks), shape, jnp.float32, lo, hi)
    d = D_MODEL
    a_in, b_in, c_in = sum(A_WIDTHS), B_SHIFT + MEM_WIDTH, sum(C_WIDTHS)
    dt = jnp.exp(unif((N_C, GDN_V_HEADS), math.log(1e-3), math.log(1e-1)))
    return {
        "x": nrm((BATCH, SEQ, d), 1.0),
        "mem": nrm((BATCH, MEM_LEN, d), 1.0),
        "attn_norm": gain((DEPTH, d)),
        "mem_norm": gain((DEPTH, d)),
        "w_mem_kv": nrm((DEPTH, d, 2 * MEM_WIDTH), d ** -0.5),
        "w_out": nrm((DEPTH, OUT_WIDTH, d), OUT_WIDTH ** -0.5),
        "ffn_norm": gain((DEPTH, d)),
        "w_ffn_up": nrm((DEPTH, d, 2 * D_FF), d ** -0.5),
        "ffn_conv": nrm((DEPTH, FFN_CONV, 2 * D_FF), FFN_CONV ** -0.5),
        "w_ffn_down": nrm((DEPTH, D_FF, d), D_FF ** -0.5),
        "final_norm": gain((d,)),
        "a_w_in": nrm((N_A, d, a_in), d ** -0.5),
        "a_sinks": nrm((N_A, SWA_Q_HEADS), 0.5),
        "b_w_in": nrm((N_B, d, b_in), d ** -0.5),
        "b_mu": unif((N_B, B_SHIFT), 0.0, 1.0),
        "b_w0": unif((N_B, MIX_WIDTH), -6.0, -1.0),
        "b_w_decay_up": nrm((N_B, RWKV_DECAY_RANK, MIX_WIDTH), 0.1 * RWKV_DECAY_RANK ** -0.5),
        "b_a0": nrm((N_B, MIX_WIDTH), 0.1),
        "b_w_iclr_up": nrm((N_B, RWKV_ICLR_RANK, MIX_WIDTH), RWKV_ICLR_RANK ** -0.5),
        "b_w_gate_up": nrm((N_B, RWKV_GATE_RANK, MIX_WIDTH), RWKV_GATE_RANK ** -0.5),
        "b_k_k": 0.85 + nrm((N_B, MIX_WIDTH), 0.02),
        "b_k_a": gain((N_B, MIX_WIDTH)),
        "b_r_k": nrm((N_B, RWKV_HEADS, RWKV_HEAD_DIM), 0.1),
        "b_gn_g": gain((N_B, MIX_WIDTH)),
        "b_gn_b": nrm((N_B, MIX_WIDTH), 0.02),
        "c_w_in": nrm((N_C, d, c_in), d ** -0.5),
        "c_conv": nrm((N_C, GDN_CONV, GDN_CONV_WIDTH), GDN_CONV ** -0.5),
        "c_a_log": jnp.log(unif((N_C, GDN_V_HEADS), 1.0, 16.0)),
        "c_dt_bias": dt + jnp.log(-jnp.expm1(-dt)),
        "c_norm_g": gain((N_C, GDN_HEAD_DIM)),
    }


def reference(x, mem, attn_norm, mem_norm, w_mem_kv, w_out, ffn_norm, w_ffn_up, ffn_conv, w_ffn_down,
              final_norm, a_w_in, a_sinks, b_w_in, b_mu, b_w0, b_w_decay_up, b_a0, b_w_iclr_up,
              b_w_gate_up, b_k_k, b_k_a, b_r_k, b_gn_g, b_gn_b, c_w_in, c_conv, c_a_log, c_dt_bias,
              c_norm_g):
    for i in range(DEPTH):
        kind, j = i % N_MIXERS, i // N_MIXERS
        h = rmsnorm(x, attn_norm[i])
        mem_kv = rmsnorm(mem, mem_norm[i]) @ w_mem_kv[i]
        if kind == 0:
            q, k, v, q_mem = split_cols(h @ a_w_in[j], A_WIDTHS)
            y = swa_sink_attention(q, k, v, a_sinks[j])
        elif kind == 1:
            p = h @ b_w_in[j]
            p_mix, q_mem = p[..., :B_SHIFT], p[..., B_SHIFT:]
            y = rwkv7_time_mix(p_mix, b_mu[j], b_w0[j], b_w_decay_up[j], b_a0[j], b_w_iclr_up[j],
                               b_w_gate_up[j], b_k_k[j], b_k_a[j], b_r_k[j], b_gn_g[j], b_gn_b[j])
        else:
            p = h @ c_w_in[j]
            p_mix, q_mem = p[..., :-MEM_WIDTH], p[..., -MEM_WIDTH:]
            y = gated_deltanet(p_mix, c_conv[j], c_a_log[j], c_dt_bias[j], c_norm_g[j])
        y_mem = memory_attention(q_mem, mem_kv)
        x = x + jnp.concatenate([y, y_mem], axis=-1) @ w_out[i]
        hf = rmsnorm(x, ffn_norm[i])
        u = causal_dwconv(hf @ w_ffn_up[i], ffn_conv[i])
        u_gate, u_val = jnp.split(u, 2, axis=-1)
        x = x + (jax.nn.silu(u_gate) * u_val) @ w_ffn_down[i]
    return rmsnorm(x, final_norm)
```

```python
import functools
import math

import jax
import jax.numpy as jnp
from jax import lax
from jax.experimental import pallas as pl
from jax.experimental.pallas import tpu as pltpu

F32 = jnp.float32
BF16 = jnp.bfloat16

D_MODEL = 2048
DEPTH = 4
MIX_WIDTH = 1536
MEM_HEADS = 4
MEM_HEAD_DIM = 128
MEM_WIDTH = 512
NORM_EPS = 1e-6
SWA_HEAD_DIM = 64
SWA_Q_HEADS = 24
SWA_KV_HEADS = 4
SWA_GROUP = 6
SWA_WINDOW = 128
RWKV_HEAD_DIM = 64
RWKV_GN_EPS = 64e-5
RWKV_CHUNK = 64
GDN_HEAD_DIM = 128
GDN_V_HEADS = 12
GDN_QK_WIDTH = 768
GDN_CONV = 4
GDN_CHUNK = 64
D_FF = 5632
FFN_CONV = 3

LANES = 128
SUBLANES = 8
VMEM_LIMIT_BYTES = 56 * 1024 * 1024


def _cparams(semantics):
    return pltpu.CompilerParams(dimension_semantics=semantics,
                                vmem_limit_bytes=VMEM_LIMIT_BYTES)


def _dot(a, b):
    return jnp.dot(a, b, preferred_element_type=F32)


def _dot_nt(a, b):
    return lax.dot_general(a, b, (((1,), (1,)), ((), ())), preferred_element_type=F32)


def _split2(x):
    hi = x.astype(BF16)
    lo = (x - hi.astype(F32)).astype(BF16)
    return hi, lo


def _split3(x):
    hi = x.astype(BF16)
    r1 = x - hi.astype(F32)
    mid = r1.astype(BF16)
    lo = (r1 - mid.astype(F32)).astype(BF16)
    return hi, mid, lo


def _dot_const_lhs(c_bf16, x):
    hi, mid, lo = _split3(x)
    return _dot(c_bf16, hi) + _dot(c_bf16, mid) + _dot(c_bf16, lo)


def _dot_nt_const_lhs(c_bf16, x):
    hi, mid, lo = _split3(x)
    return _dot_nt(c_bf16, hi) + _dot_nt(c_bf16, mid) + _dot_nt(c_bf16, lo)


def _dot_tn(a, b):
    return lax.dot_general(a, b, (((0,), (0,)), ((), ())), preferred_element_type=F32)


def _dot_const_rhs(x, c_bf16):
    hi, mid, lo = _split3(x)
    return _dot(hi, c_bf16) + _dot(mid, c_bf16) + _dot(lo, c_bf16)


def _dot3(a, b):
    ah, al = _split2(a)
    bh, bl = _split2(b)
    return _dot(ah, bh) + _dot(ah, bl) + _dot(al, bh)


def _sigmoid(x):
    return 1.0 / (1.0 + jnp.exp(-x))


def _silu(x):
    return x * _sigmoid(x)


def _softplus(x):
    return jnp.maximum(x, 0.0) + jnp.log(1.0 + jnp.exp(-jnp.abs(x)))


def _group_sum(x, group_ones):
    cols = []
    for c in range(x.shape[1] // LANES):
        cols.append(_dot_const_rhs(x[:, c * LANES:(c + 1) * LANES], group_ones))
    return cols[0] if len(cols) == 1 else jnp.concatenate(cols, axis=1)


def _group_ones(group):
    i = lax.broadcasted_iota(jnp.int32, (LANES, LANES), 0) // group
    j = lax.broadcasted_iota(jnp.int32, (LANES, LANES), 1) // group
    return (i == j).astype(BF16)


def _shift_rows(x, halo, s, seq_start):
    rolled = pltpu.roll(x, s, 0)
    hr = pltpu.roll(halo, s, 0)
    hr = jnp.where(seq_start, 0.0, hr)
    rows = lax.broadcasted_iota(jnp.int32, hr.shape, 0)
    head = jnp.where(rows < s, hr, rolled[:SUBLANES])
    return jnp.concatenate([head, rolled[SUBLANES:]], axis=0)


def _rmsnorm_kernel(x_ref, g_ref, o_ref):
    x = x_ref[...]
    ms = jnp.mean(x * x, axis=-1, keepdims=True)
    o_ref[...] = (x * lax.rsqrt(ms + NORM_EPS) * g_ref[...]).astype(o_ref.dtype)


def rmsnorm(x, g, out_dtype, tm=512):
    t, d = x.shape
    tm = min(tm, t)
    return pl.pallas_call(
        _rmsnorm_kernel,
        grid=(t // tm,),
        in_specs=[pl.BlockSpec((tm, d), lambda i: (i, 0)),
                  pl.BlockSpec((1, d), lambda i: (0, 0))],
        out_specs=pl.BlockSpec((tm, d), lambda i: (i, 0)),
        out_shape=jax.ShapeDtypeStruct((t, d), out_dtype),
        compiler_params=_cparams(("parallel",)),
    )(x, g.reshape(1, d))


def _matmul_kernel(a_ref, w_ref, o_ref):
    o_ref[...] = _dot(a_ref[...], w_ref[...]).astype(o_ref.dtype)


def matmul(a, w, tm, tn, out_dtype=F32):
    m, k = a.shape
    n = w.shape[1]
    tm = min(tm, m)
    return pl.pallas_call(
        _matmul_kernel,
        grid=(m // tm, n // tn),
        in_specs=[pl.BlockSpec((tm, k), lambda i, j: (i, 0)),
                  pl.BlockSpec((k, tn), lambda i, j: (0, j))],
        out_specs=pl.BlockSpec((tm, tn), lambda i, j: (i, j)),
        out_shape=jax.ShapeDtypeStruct((m, n), out_dtype),
        compiler_params=_cparams(("parallel", "parallel")),
    )(a, w)


def _outproj_kernel(y_ref, ym_ref, w1_ref, w2_ref, x_ref, o_ref):
    acc = _dot(y_ref[...], w1_ref[...]) + _dot(ym_ref[...], w2_ref[...])
    o_ref[...] = x_ref[...] + acc


def outproj(y, ym, w1, w2, x, tm=512, tn=1024):
    m, k1 = y.shape
    k2 = ym.shape[1]
    n = w1.shape[1]
    tm = min(tm, m)
    tn = min(tn, n)
    return pl.pallas_call(
        _outproj_kernel,
        grid=(m // tm, n // tn),
        in_specs=[pl.BlockSpec((tm, k1), lambda i, j: (i, 0)),
                  pl.BlockSpec((tm, k2), lambda i, j: (i, 0)),
                  pl.BlockSpec((k1, tn), lambda i, j: (0, j)),
                  pl.BlockSpec((k2, tn), lambda i, j: (0, j)),
                  pl.BlockSpec((tm, tn), lambda i, j: (i, j))],
        out_specs=pl.BlockSpec((tm, tn), lambda i, j: (i, j)),
        out_shape=jax.ShapeDtypeStruct((m, n), F32),
        compiler_params=_cparams(("parallel", "parallel")),
    )(y, ym, w1, w2, x)


def _ffn_kernel(h_ref, hh_ref, wg_ref, wv_ref, cg_ref, cv_ref, wd_ref, x_ref, o_ref, acc_ref,
                *, tm, seq):
    i = pl.program_id(0)
    j = pl.program_id(1)
    seq_start = (i * tm) % seq == 0

    @pl.when(j == 0)
    def _():
        acc_ref[...] = jnp.zeros_like(acc_ref)

    h = h_ref[...]
    hh = hh_ref[...]

    def branch(w_ref, c_ref):
        u = _dot(h, w_ref[...])
        uh = _dot(hh, w_ref[...])[SUBLANES:]
        c = c_ref[...]
        return (c[0:1] * _shift_rows(u, uh, 2, seq_start)
                + c[1:2] * _shift_rows(u, uh, 1, seq_start)
                + c[2:3] * u)

    act = _silu(branch(wg_ref, cg_ref)) * branch(wv_ref, cv_ref)
    acc_ref[...] += _dot(act.astype(BF16), wd_ref[...])

    @pl.when(j == pl.num_programs(1) - 1)
    def _():
        o_ref[...] = x_ref[...] + acc_ref[...]


def ffn(h, w_up, conv, w_down, x, seq, tm=512, tf=512):
    t, d = h.shape
    f = w_down.shape[0]
    tm = min(tm, seq)
    tf = min(tf, f)
    nf = f // tf
    hb = 2 * SUBLANES
    kern = functools.partial(_ffn_kernel, tm=tm, seq=seq)
    return pl.pallas_call(
        kern,
        grid=(t // tm, nf),
        in_specs=[
            pl.BlockSpec((tm, d), lambda i, j: (i, 0)),
            pl.BlockSpec((hb, d), lambda i, j: (jnp.maximum(i * (tm // hb) - 1, 0), 0)),
            pl.BlockSpec((d, tf), lambda i, j: (0, j)),
            pl.BlockSpec((d, tf), lambda i, j: (0, nf + j)),
            pl.BlockSpec((FFN_CONV, tf), lambda i, j: (0, j)),
            pl.BlockSpec((FFN_CONV, tf), lambda i, j: (0, nf + j)),
            pl.BlockSpec((tf, d), lambda i, j: (j, 0)),
            pl.BlockSpec((tm, d), lambda i, j: (i, 0)),
        ],
        out_specs=pl.BlockSpec((tm, d), lambda i, j: (i, 0)),
        out_shape=jax.ShapeDtypeStruct((t, d), F32),
        scratch_shapes=[pltpu.VMEM((tm, d), F32)],
        compiler_params=_cparams(("parallel", "arbitrary")),
    )(h, h, w_up, w_up, conv, conv, w_down, x)


def _swa_kernel(sink_ref, q_ref, kp_ref, kc_ref, vp_ref, vc_ref, o_ref, *, blk):
    n = pl.program_id(1)
    dh = SWA_HEAD_DIM
    t = lax.broadcasted_iota(jnp.int32, (blk, 2 * blk), 0)
    j = lax.broadcasted_iota(jnp.int32, (blk, 2 * blk), 1)
    dist = t - j + blk
    valid = (dist >= 0) & (dist < SWA_WINDOW) & ((j >= blk) | (n > 0))
    distf = dist.astype(F32)
    scale = dh ** -0.5
    for kh in range(SWA_KV_HEADS):
        ks = slice(kh * dh, (kh + 1) * dh)
        k = jnp.concatenate([kp_ref[0, :, ks], kc_ref[0, :, ks]], axis=0).astype(BF16)
        v = jnp.concatenate([vp_ref[0, :, ks], vc_ref[0, :, ks]], axis=0).astype(BF16)
        for g in range(SWA_GROUP):
            h = kh * SWA_GROUP + g
            slope = 2.0 ** (-8.0 * (h + 1.0) / SWA_Q_HEADS)
            sink = sink_ref[h]
            q = q_ref[0, :, h * dh:(h + 1) * dh].astype(BF16)
            s = _dot_nt(q, k) * scale - slope * distf
            s = jnp.where(valid, s, -1e30)
            m = jnp.maximum(jnp.max(s, axis=-1, keepdims=True), sink)
            p = jnp.exp(s - m)
            denom = jnp.sum(p, axis=-1, keepdims=True) + jnp.exp(sink - m)
            o = _dot(p.astype(BF16), v) / denom
            o_ref[0, :, h * dh:(h + 1) * dh] = o.astype(o_ref.dtype)


def swa_attention(p3, sinks, q_col, k_col, v_col):
    b, s, _ = p3.shape
    blk = SWA_WINDOW
    kvw = SWA_KV_HEADS * SWA_HEAD_DIM
    kern = functools.partial(_swa_kernel, blk=blk)
    prev = lambda n: jnp.maximum(n - 1, 0)
    return pl.pallas_call(
        kern,
        grid=(b, s // blk),
        in_specs=[
            pl.BlockSpec(memory_space=pltpu.SMEM),
            pl.BlockSpec((1, blk, MIX_WIDTH), lambda bi, n: (bi, n, q_col // MIX_WIDTH)),
            pl.BlockSpec((1, blk, kvw), lambda bi, n: (bi, prev(n), k_col // kvw)),
            pl.BlockSpec((1, blk, kvw), lambda bi, n: (bi, n, k_col // kvw)),
            pl.BlockSpec((1, blk, kvw), lambda bi, n: (bi, prev(n), v_col // kvw)),
            pl.BlockSpec((1, blk, kvw), lambda bi, n: (bi, n, v_col // kvw)),
        ],
        out_specs=pl.BlockSpec((1, blk, MIX_WIDTH), lambda bi, n: (bi, n, 0)),
        out_shape=jax.ShapeDtypeStruct((b, s, MIX_WIDTH), BF16),
        compiler_params=_cparams(("parallel", "parallel")),
    )(sinks, p3, p3, p3, p3, p3)


def _memattn_kernel(q_ref, kv_ref, o_ref):
    dh = MEM_HEAD_DIM
    scale = dh ** -0.5
    for h in range(MEM_HEADS):
        q = q_ref[0, :, h * dh:(h + 1) * dh].astype(BF16)
        k = kv_ref[0, :, h * dh:(h + 1) * dh].astype(BF16)
        v = kv_ref[0, :, MEM_WIDTH + h * dh:MEM_WIDTH + (h + 1) * dh].astype(BF16)
        s = _dot_nt(q, k) * scale
        m = jnp.max(s, axis=-1, keepdims=True)
        p = jnp.exp(s - m)
        denom = jnp.sum(p, axis=-1, keepdims=True)
        o = _dot(p.astype(BF16), v) / denom
        o_ref[0, :, h * dh:(h + 1) * dh] = o.astype(o_ref.dtype)


def mem_attention(p3, q_col, mem_kv3, tq=512):
    b, s, _ = p3.shape
    m = mem_kv3.shape[1]
    tq = min(tq, s)
    return pl.pallas_call(
        _memattn_kernel,
        grid=(b, s // tq),
        in_specs=[
            pl.BlockSpec((1, tq, MEM_WIDTH), lambda bi, i: (bi, i, q_col // MEM_WIDTH)),
            pl.BlockSpec((1, m, 2 * MEM_WIDTH), lambda bi, i: (bi, 0, 0)),
        ],
        out_specs=pl.BlockSpec((1, tq, MEM_WIDTH), lambda bi, i: (bi, i, 0)),
        out_shape=jax.ShapeDtypeStruct((b, s, MEM_WIDTH), BF16),
        compiler_params=_cparams(("parallel", "parallel")),
    )(p3, mem_kv3)


RW_RKV = 0
RW_QMEM = 3 * MIX_WIDTH
RW_LR = RW_QMEM + MEM_WIDTH
RW_LR_WIDTH = 512
RW_N = RW_LR + RW_LR_WIDTH


def _rwkv_pre_kernel(rkv_ref, lr_ref, rkv_h_ref, lr_h_ref, mu_rkv_ref, mu_lr_ref,
                     w0_ref, wdu_ref, a0_ref, wiu_ref, wgu_ref, kk_ref, ka_ref,
                     r_o, k_o, v_o, a_o, b_o, lw_o, g_o, *, tm, seq):
    i = pl.program_id(0)
    seq_start = (i * tm) % seq == 0

    def mixed(x_ref, h_ref, mu_ref):
        x = x_ref[...]
        prev = _shift_rows(x, h_ref[...], 1, seq_start)
        return x + (prev - x) * mu_ref[...]

    rkv = mixed(rkv_ref, rkv_h_ref, mu_rkv_ref)
    lr = mixed(lr_ref, lr_h_ref, mu_lr_ref)
    w = MIX_WIDTH
    r = rkv[:, :w]
    k = rkv[:, w:2 * w]
    v = rkv[:, 2 * w:]
    wd = jnp.tanh(lr[:, 0:LANES]).astype(BF16)
    ad = lr[:, LANES:2 * LANES].astype(BF16)
    gd = _sigmoid(lr[:, 2 * LANES:]).astype(BF16)
    z = w0_ref[...] + _dot(wd, wdu_ref[...])
    w_log = -_softplus(-z) - 0.5
    lw = -jnp.exp(w_log)
    a_sig = _sigmoid(a0_ref[...] + _dot(ad, wiu_ref[...]))
    g = _dot(gd, wgu_ref[...])
    ones64 = _group_ones(RWKV_HEAD_DIM)
    kk = k * kk_ref[...]
    kk = kk * lax.rsqrt(_group_sum(kk * kk, ones64) + 1e-6)
    k2 = k * (1.0 + (a_sig - 1.0) * ka_ref[...])
    r_o[...] = r
    k_o[...] = k2
    v_o[...] = v
    a_o[...] = kk
    b_o[...] = kk * a_sig
    lw_o[...] = lw
    g_o[...] = g


def rwkv_pre(p, mu_rkv, mu_lr, w0, wdu, a0, wiu, wgu, k_k, k_a, seq, tm=256):
    t = p.shape[0]
    tm = min(tm, seq)
    w = MIX_WIDTH
    hb = SUBLANES
    kern = functools.partial(_rwkv_pre_kernel, tm=tm, seq=seq)
    row = lambda i: (i, 0)
    halo = lambda i: jnp.maximum(i * (tm // hb) - 1, 0)
    const = lambda i: (0, 0)
    vec = pl.BlockSpec((1, w), const)
    out = pl.BlockSpec((tm, w), row)
    return pl.pallas_call(
        kern,
        grid=(t // tm,),
        in_specs=[
            pl.BlockSpec((tm, 3 * w), lambda i: (i, 0)),
            pl.BlockSpec((tm, RW_LR_WIDTH), lambda i: (i, RW_LR // RW_LR_WIDTH)),
            pl.BlockSpec((hb, 3 * w), lambda i: (halo(i), 0)),
            pl.BlockSpec((hb, RW_LR_WIDTH), lambda i: (halo(i), RW_LR // RW_LR_WIDTH)),
            pl.BlockSpec((1, 3 * w), const),
            pl.BlockSpec((1, RW_LR_WIDTH), const),
            vec,
            pl.BlockSpec((LANES, w), const),
            vec,
            pl.BlockSpec((LANES, w), const),
            pl.BlockSpec((2 * LANES, w), const),
            vec, vec,
        ],
        out_specs=[out] * 7,
        out_shape=[jax.ShapeDtypeStruct((t, w), F32)] * 7,
        compiler_params=_cparams(("parallel",)),
    )(p, p, p, p, mu_rkv, mu_lr, w0, wdu, a0, wiu, wgu, k_k, k_a)


def _expand(x, lane_a):
    return jnp.concatenate([jnp.where(lane_a, x, 0.0), jnp.where(lane_a, 0.0, x)], axis=0)


def _unit_lower_inverse_wide(x, lane_a, nsteps):
    c = x.shape[0]
    rows = lax.broadcasted_iota(jnp.int32, x.shape, 0)
    cols = lax.broadcasted_iota(jnp.int32, x.shape, 1) % c
    t = jnp.where(rows == cols, 1.0, 0.0) + x
    pw = x
    for _ in range(nsteps):
        pw = _dot3(pw, _expand(pw, lane_a))
        t = t + _dot3(t, _expand(pw, lane_a))
    return t


def _rwkv_scan_kernel(r_ref, k_ref, v_ref, a_ref, b_ref, lw_ref, g_ref,
                      rk_ref, gng_ref, gnb_ref, o_ref, state_ref, *, chunk, npairs):
    c = chunk
    hd = RWKV_HEAD_DIM

    @pl.when(pl.program_id(2) == 0)
    def _():
        state_ref[...] = jnp.zeros_like(state_ref)

    ri = lax.broadcasted_iota(jnp.int32, (c, c), 0)
    ci = lax.broadcasted_iota(jnp.int32, (c, c), 1)
    tri = (ri >= ci).astype(BF16)
    rows = lax.broadcasted_iota(jnp.int32, (c, LANES), 0)
    lanes = lax.broadcasted_iota(jnp.int32, (c, LANES), 1)
    lane_a = lanes < hd
    strict = (lanes % hd) < rows
    incl = (lanes % hd) <= rows
    bi = lax.broadcasted_iota(jnp.int32, (LANES, LANES), 0) // hd
    bj = lax.broadcasted_iota(jnp.int32, (LANES, LANES), 1) // hd
    blockdiag = bi == bj
    ones64 = blockdiag.astype(BF16)

    for p in range(npairs):
        sl = slice(p * LANES, (p + 1) * LANES)
        r = r_ref[0, :, sl]
        k = k_ref[0, :, sl]
        v = v_ref[0, :, sl]
        a = a_ref[0, :, sl]
        b = b_ref[0, :, sl]
        lw = lw_ref[0, :, sl]
        cs = _dot_const_lhs(tri, lw)
        c_last = cs[c - 1:c, :]
        a_t = a * jnp.exp(cs - lw)
        r_t = r * jnp.exp(cs)
        e_neg = jnp.exp(-cs)
        k_t = k * e_neg
        b_t = b * e_neg
        e_end = jnp.exp(c_last - cs)
        k_end = k * e_end
        b_end = b * e_end
        ar = jnp.concatenate([a_t, r_t], axis=0).astype(BF16)
        s_k = _dot_nt(ar, _expand(k_t, lane_a).astype(BF16))
        s_b = _dot_nt(ar, _expand(b_t, lane_a).astype(BF16))
        l_ak = jnp.where(strict, s_k[:c], 0.0)
        m_rk = jnp.where(incl, s_k[c:], 0.0)
        l_ab = jnp.where(strict, s_b[:c], 0.0)
        m_rb = jnp.where(incl, s_b[c:], 0.0)
        tinv = _unit_lower_inverse_wide(-l_ab, lane_a, int(math.log2(c)) - 1)
        state = state_ref[p]
        ah = _dot_nt(ar, state.astype(BF16))
        v_x = _expand(v, lane_a)
        rhs = ah[:c] + _dot(l_ak.astype(BF16), v_x.astype(BF16))
        u = _dot3(tinv, _expand(rhs, lane_a))
        u_x = _expand(u, lane_a)
        y = (ah[c:] + _dot(m_rk.astype(BF16), v_x.astype(BF16))
             - _dot(m_rb.astype(BF16), u_x.astype(BF16)))
        vu = jnp.concatenate([v, -u], axis=0)
        kb_end = jnp.concatenate([k_end, b_end], axis=0)
        upd = _dot_tn(vu.astype(BF16), kb_end.astype(BF16))
        state_ref[p] = jnp.where(blockdiag, state * jnp.exp(c_last) + upd, 0.0)

        mean = _group_sum(y, ones64) * (1.0 / hd)
        yc = y - mean
        var = _group_sum(yc * yc, ones64) * (1.0 / hd)
        yn = yc * lax.rsqrt(var + RWKV_GN_EPS) * gng_ref[0:1, sl] + gnb_ref[0:1, sl]
        bonus = _group_sum(r * k * rk_ref[0:1, sl], ones64) * v
        o_ref[0, :, sl] = ((yn + bonus) * g_ref[0, :, sl]).astype(o_ref.dtype)


def rwkv_scan(r, k, v, a, b, lw, g, r_k, gn_g, gn_b, batch, seq, lane_block=512):
    w = MIX_WIDTH
    c = RWKV_CHUNK
    lane_block = min(lane_block, w)
    npairs = lane_block // LANES
    ngroups = w // lane_block
    shp = (batch, seq, w)
    args = [z.reshape(shp) for z in (r, k, v, a, b, lw, g)]
    tok = pl.BlockSpec((1, c, lane_block), lambda bi, gi, n: (bi, n, gi))
    vec = pl.BlockSpec((1, lane_block), lambda bi, gi, n: (0, gi))
    kern = functools.partial(_rwkv_scan_kernel, chunk=c, npairs=npairs)
    return pl.pallas_call(
        kern,
        grid=(batch, ngroups, seq // c),
        in_specs=[tok] * 7 + [vec] * 3,
        out_specs=tok,
        out_shape=jax.ShapeDtypeStruct(shp, BF16),
        scratch_shapes=[pltpu.VMEM((npairs, LANES, LANES), F32)],
        compiler_params=_cparams(("parallel", "parallel", "arbitrary")),
    )(*args, r_k, gn_g, gn_b)


GD_QKV = 0
GD_QKV_WIDTH = 2 * GDN_QK_WIDTH + MIX_WIDTH
GD_Z = GD_QKV_WIDTH
GD_QMEM = GD_Z + MIX_WIDTH
GD_BA = GD_QMEM + MEM_WIDTH
GD_N = 5376


def _gdn_pre_kernel(x_ref, ba_ref, xh_ref, cw_ref, alog_ref, dtb_ref,
                    q_o, k_o, v_o, gb_o, *, tm, seq):
    i = pl.program_id(0)
    seq_start = (i * tm) % seq == 0
    x = x_ref[...]
    halo = xh_ref[...]
    cw = cw_ref[...]
    y = cw[GDN_CONV - 1:GDN_CONV] * x
    for s in range(1, GDN_CONV):
        y = y + cw[GDN_CONV - 1 - s:GDN_CONV - s] * _shift_rows(x, halo, s, seq_start)
    y = _silu(y)
    ones128 = jnp.ones((LANES, LANES), BF16)
    qw = GDN_QK_WIDTH
    q = y[:, :qw]
    k = y[:, qw:2 * qw]
    q = q * lax.rsqrt(_group_sum(q * q, ones128) + 1e-6) * (GDN_HEAD_DIM ** -0.5)
    k = k * lax.rsqrt(_group_sum(k * k, ones128) + 1e-6)
    q_o[...] = q
    k_o[...] = k
    v_o[...] = y[:, 2 * qw:]
    ba = ba_ref[...]
    lanes = lax.broadcasted_iota(jnp.int32, ba.shape, 1)
    beta = _sigmoid(ba)
    g = -jnp.exp(alog_ref[...]) * _softplus(ba + dtb_ref[...])
    gb_o[...] = jnp.where(lanes < GDN_V_HEADS, beta, g)


def gdn_pre(p, conv_w, alog_row, dtb_row, seq, tm=256):
    t = p.shape[0]
    tm = min(tm, seq)
    hb = SUBLANES
    kern = functools.partial(_gdn_pre_kernel, tm=tm, seq=seq)
    row = lambda i: (i, 0)
    const = lambda i: (0, 0)
    halo = lambda i: jnp.maximum(i * (tm // hb) - 1, 0)
    return pl.pallas_call(
        kern,
        grid=(t // tm,),
        in_specs=[
            pl.BlockSpec((tm, GD_QKV_WIDTH), row),
            pl.BlockSpec((tm, LANES), lambda i: (i, GD_BA // LANES)),
            pl.BlockSpec((hb, GD_QKV_WIDTH), lambda i: (halo(i), 0)),
            pl.BlockSpec((GDN_CONV, GD_QKV_WIDTH), const),
            pl.BlockSpec((1, LANES), const),
            pl.BlockSpec((1, LANES), const),
        ],
        out_specs=[pl.BlockSpec((tm, GDN_QK_WIDTH), row),
                   pl.BlockSpec((tm, GDN_QK_WIDTH), row),
                   pl.BlockSpec((tm, MIX_WIDTH), row),
                   pl.BlockSpec((tm, LANES), row)],
        out_shape=[jax.ShapeDtypeStruct((t, GDN_QK_WIDTH), F32),
                   jax.ShapeDtypeStruct((t, GDN_QK_WIDTH), F32),
                   jax.ShapeDtypeStruct((t, MIX_WIDTH), F32),
                   jax.ShapeDtypeStruct((t, LANES), F32)],
        compiler_params=_cparams(("parallel",)),
    )(p, p, p, conv_w, alog_row, dtb_row)


def _unit_lower_inverse(x, nsteps):
    c = x.shape[0]
    ri = lax.broadcasted_iota(jnp.int32, (c, c), 0)
    ci = lax.broadcasted_iota(jnp.int32, (c, c), 1)
    t = jnp.where(ri == ci, 1.0, 0.0) + x
    pw = x
    for _ in range(nsteps):
        pw = _dot3(pw, pw)
        t = t + _dot3(t, pw)
    return t


def _gdn_scan_kernel(q_ref, k_ref, v_ref, gb_ref, z_ref, ng_ref, o_ref, state_ref, *, chunk):
    c = chunk
    dh = GDN_HEAD_DIM

    @pl.when(pl.program_id(1) == 0)
    def _():
        state_ref[...] = jnp.zeros_like(state_ref)

    ri = lax.broadcasted_iota(jnp.int32, (c, c), 0)
    ci = lax.broadcasted_iota(jnp.int32, (c, c), 1)
    tri = (ri >= ci).astype(BF16)
    causal = ri >= ci
    strict = ri > ci
    lane0 = (lax.broadcasted_iota(jnp.int32, (c, LANES), 1) == 0).astype(BF16)
    gb = gb_ref[0]
    rep = GDN_V_HEADS // (GDN_QK_WIDTH // dh)
    for h in range(GDN_V_HEADS):
        hq = h // rep
        q = q_ref[0, :, hq * dh:(hq + 1) * dh]
        k = k_ref[0, :, hq * dh:(hq + 1) * dh]
        v = v_ref[0, :, h * dh:(h + 1) * dh]
        beta = jnp.broadcast_to(gb[:, h:h + 1], (c, dh))
        g = jnp.broadcast_to(gb[:, GDN_V_HEADS + h:GDN_V_HEADS + h + 1], (c, dh))
        gc = _dot_const_lhs(tri, g)
        gcs = gc[:, :c]
        gcr = _dot_nt_const_lhs(lane0, gc)
        diff = gcs - gcr
        decay = jnp.where(causal, jnp.exp(jnp.minimum(diff, 0.0)), 0.0)
        kb = k * beta
        kbf = k.astype(BF16)
        lmat = jnp.where(strict, _dot_nt(kb.astype(BF16), kbf) * decay, 0.0)
        tinv = _unit_lower_inverse(-lmat, int(math.log2(c)) - 1)
        e_gc = jnp.exp(gc)
        u = _dot3(tinv, v * beta)
        w = _dot3(tinv, kb * e_gc)
        a_qk = _dot_nt(q.astype(BF16), kbf) * decay
        q_dec = q * e_gc
        g_last = gc[c - 1:c, :]
        k_dec = k * jnp.exp(g_last - gc)
        state = state_ref[h]
        sb = state.astype(BF16)
        v_new = u - _dot(w.astype(BF16), sb)
        out = _dot(q_dec.astype(BF16), sb) + _dot(a_qk.astype(BF16), v_new.astype(BF16))
        state_ref[h] = state * jnp.exp(g_last) + _dot_tn(k_dec.astype(BF16), v_new.astype(BF16))
        ms = jnp.mean(out * out, axis=-1, keepdims=True)
        o = out * lax.rsqrt(ms + NORM_EPS) * ng_ref[...]
        o = o * _silu(z_ref[0, :, h * dh:(h + 1) * dh])
        o_ref[0, :, h * dh:(h + 1) * dh] = o.astype(o_ref.dtype)


def gdn_scan(q, k, v, gb, p3, norm_g, batch, seq):
    c = GDN_CHUNK
    qw = GDN_QK_WIDTH
    w = MIX_WIDTH
    kern = functools.partial(_gdn_scan_kernel, chunk=c)
    tokq = pl.BlockSpec((1, c, qw), lambda bi, n: (bi, n, 0))
    tokv = pl.BlockSpec((1, c, w), lambda bi, n: (bi, n, 0))
    return pl.pallas_call(
        kern,
        grid=(batch, seq // c),
        in_specs=[tokq, tokq, tokv,
                  pl.BlockSpec((1, c, LANES), lambda bi, n: (bi, n, 0)),
                  pl.BlockSpec((1, c, w), lambda bi, n: (bi, n, GD_Z // w)),
                  pl.BlockSpec((1, GDN_HEAD_DIM), lambda bi, n: (0, 0))],
        out_specs=tokv,
        out_shape=jax.ShapeDtypeStruct((batch, seq, w), BF16),
        scratch_shapes=[pltpu.VMEM((GDN_V_HEADS, GDN_HEAD_DIM, GDN_HEAD_DIM), F32)],
        compiler_params=_cparams(("parallel", "arbitrary")),
    )(q.reshape(batch, seq, qw), k.reshape(batch, seq, qw), v.reshape(batch, seq, w),
      gb.reshape(batch, seq, LANES), p3, norm_g.reshape(1, GDN_HEAD_DIM))


def _pad_cols(w, n):
    return jnp.pad(w, ((0, 0), (0, n - w.shape[1])))


def _pad_rows(w, n):
    return jnp.pad(w, ((0, n - w.shape[0]), (0, 0)))


def _mixer_a(h, w_in, sinks, batch, seq):
    p = matmul(h, w_in.astype(BF16), tm=512, tn=1280)
    p3 = p.reshape(batch, seq, -1)
    y = swa_attention(p3, sinks, q_col=0, k_col=MIX_WIDTH,
                      v_col=MIX_WIDTH + SWA_KV_HEADS * SWA_HEAD_DIM)
    return y, p3, 2048


def _mixer_b(h, w_in, mu, w0, w_decay_up, a0, w_iclr_up, w_gate_up, k_k, k_a, r_k, gn_g, gn_b,
             batch, seq):
    w = MIX_WIDTH
    dr = w_decay_up.shape[0]
    ir = w_iclr_up.shape[0]
    gr = w_gate_up.shape[0]
    o_wd, o_ad, o_gd, o_qm = 3 * w, 3 * w + dr, 3 * w + dr + ir, 3 * w + dr + ir + gr
    w_in_r = jnp.concatenate([
        w_in[:, :3 * w], w_in[:, o_qm:],
        _pad_cols(w_in[:, o_wd:o_ad], LANES), _pad_cols(w_in[:, o_ad:o_gd], LANES),
        w_in[:, o_gd:o_qm]], axis=1).astype(BF16)
    mu_rkv = mu[:3 * w].reshape(1, -1)
    mu_lr = jnp.concatenate([jnp.pad(mu[o_wd:o_ad], (0, LANES - dr)),
                             jnp.pad(mu[o_ad:o_gd], (0, LANES - ir)),
                             mu[o_gd:o_qm]]).reshape(1, -1)
    p = matmul(h, w_in_r, tm=512, tn=512)
    row = lambda z: z.reshape(1, -1)
    r, k, v, a, b, lw, g = rwkv_pre(
        p, mu_rkv, mu_lr, row(w0), _pad_rows(w_decay_up, LANES).astype(BF16), row(a0),
        _pad_rows(w_iclr_up, LANES).astype(BF16), w_gate_up.astype(BF16), row(k_k), row(k_a), seq)
    y = rwkv_scan(r, k, v, a, b, lw, g, row(r_k), row(gn_g), row(gn_b), batch, seq)
    return y, p.reshape(batch, seq, -1), RW_QMEM


def _mixer_c(h, w_in, conv_w, a_log, dt_bias, norm_g, batch, seq):
    w = MIX_WIDTH
    nh = GDN_V_HEADS
    o_z = GD_QKV_WIDTH
    o_bt = o_z + w
    o_at = o_bt + nh
    o_qm = o_at + nh
    ba = w_in[:, o_bt:o_qm]
    w_in_r = jnp.concatenate([
        w_in[:, :o_z], w_in[:, o_z:o_bt], w_in[:, o_qm:], _pad_cols(ba, LANES)], axis=1)
    w_in_r = _pad_cols(w_in_r, GD_N).astype(BF16)
    p = matmul(h, w_in_r, tm=512, tn=768)
    pad = LANES - 2 * nh
    alog_row = jnp.concatenate([jnp.zeros((nh,), F32), a_log, jnp.zeros((pad,), F32)]).reshape(1, -1)
    dtb_row = jnp.concatenate([jnp.zeros((nh,), F32), dt_bias, jnp.zeros((pad,), F32)]).reshape(1, -1)
    q, k, v, gb = gdn_pre(p, conv_w, alog_row, dtb_row, seq)
    p3 = p.reshape(batch, seq, -1)
    y = gdn_scan(q, k, v, gb, p3, norm_g, batch, seq)
    return y, p3, GD_QMEM


def kernel(x, mem, attn_norm, mem_norm, w_mem_kv, w_out, ffn_norm, w_ffn_up, ffn_conv, w_ffn_down,
           final_norm, a_w_in, a_sinks, b_w_in, b_mu, b_w0, b_w_decay_up, b_a0, b_w_iclr_up,
           b_w_gate_up, b_k_k, b_k_a, b_r_k, b_gn_g, b_gn_b, c_w_in, c_conv, c_a_log, c_dt_bias,
           c_norm_g):
    batch, seq, d = x.shape
    mlen = mem.shape[1]
    depth = attn_norm.shape[0]
    x = x.reshape(batch * seq, d)
    mem2 = mem.reshape(batch * mlen, d)
    for i in range(depth):
        kind, j = i % 3, i // 3
        h = rmsnorm(x, attn_norm[i], BF16)
        hm = rmsnorm(mem2, mem_norm[i], BF16)
        mem_kv = matmul(hm, w_mem_kv[i].astype(BF16), tm=512, tn=1024)
        mem_kv3 = mem_kv.reshape(batch, mlen, -1)
        if kind == 0:
            y, p3, q_col = _mixer_a(h, a_w_in[j], a_sinks[j], batch, seq)
        elif kind == 1:
            y, p3, q_col = _mixer_b(h, b_w_in[j], b_mu[j], b_w0[j], b_w_decay_up[j], b_a0[j],
                                    b_w_iclr_up[j], b_w_gate_up[j], b_k_k[j], b_k_a[j],
                                    b_r_k[j].reshape(-1), b_gn_g[j], b_gn_b[j], batch, seq)
        else:
            y, p3, q_col = _mixer_c(h, c_w_in[j], c_conv[j], c_a_log[j], c_dt_bias[j],
                                    c_norm_g[j], batch, seq)
        y_mem = mem_attention(p3, q_col, mem_kv3)
        wo = w_out[i].astype(BF16)
        x = outproj(y.reshape(batch * seq, -1), y_mem.reshape(batch * seq, -1),
                    wo[:MIX_WIDTH], wo[MIX_WIDTH:], x)
        hf = rmsnorm(x, ffn_norm[i], BF16)
        x = ffn(hf, w_ffn_up[i].astype(BF16), ffn_conv[i], w_ffn_down[i].astype(BF16), x, seq)
    out = rmsnorm(x, final_norm, F32)
    return out.reshape(batch, seq, d)
```

```python
import functools
import math

import jax
import jax.numpy as jnp
from jax import lax
from jax.experimental import pallas as pl
from jax.experimental.pallas import tpu as pltpu

F32 = jnp.float32
BF16 = jnp.bfloat16

D_MODEL = 2048
DEPTH = 4
MIX_WIDTH = 1536
MEM_HEADS = 4
MEM_HEAD_DIM = 128
MEM_WIDTH = 512
NORM_EPS = 1e-6
SWA_HEAD_DIM = 64
SWA_Q_HEADS = 24
SWA_KV_HEADS = 4
SWA_GROUP = 6
SWA_WINDOW = 128
RWKV_HEAD_DIM = 64
RWKV_GN_EPS = 64e-5
RWKV_CHUNK = 64
GDN_HEAD_DIM = 128
GDN_V_HEADS = 12
GDN_QK_WIDTH = 768
GDN_CONV = 4
GDN_CHUNK = 64
D_FF = 5632
FFN_CONV = 3

LANES = 128
SUBLANES = 8
VMEM_LIMIT_BYTES = 56 * 1024 * 1024


def _cparams(semantics):
    return pltpu.CompilerParams(dimension_semantics=semantics,
                                vmem_limit_bytes=VMEM_LIMIT_BYTES)


def _dot(a, b):
    return jnp.dot(a, b, preferred_element_type=F32)


def _dot_nt(a, b):
    return lax.dot_general(a, b, (((1,), (1,)), ((), ())), preferred_element_type=F32)


def _split2(x):
    hi = x.astype(BF16)
    lo = (x - hi.astype(F32)).astype(BF16)
    return hi, lo


def _split3(x):
    hi = x.astype(BF16)
    r1 = x - hi.astype(F32)
    mid = r1.astype(BF16)
    lo = (r1 - mid.astype(F32)).astype(BF16)
    return hi, mid, lo


def _dot_const_lhs(c_bf16, x):
    hi, mid, lo = _split3(x)
    return _dot(c_bf16, hi) + _dot(c_bf16, mid) + _dot(c_bf16, lo)


def _dot_nt_const_lhs(c_bf16, x):
    hi, mid, lo = _split3(x)
    return _dot_nt(c_bf16, hi) + _dot_nt(c_bf16, mid) + _dot_nt(c_bf16, lo)


def _dot_tn(a, b):
    return lax.dot_general(a, b, (((0,), (0,)), ((), ())), preferred_element_type=F32)


def _dot_const_rhs(x, c_bf16):
    hi, mid, lo = _split3(x)
    return _dot(hi, c_bf16) + _dot(mid, c_bf16) + _dot(lo, c_bf16)


def _dot3(a, b):
    ah, al = _split2(a)
    bh, bl = _split2(b)
    return _dot(ah, bh) + _dot(ah, bl) + _dot(al, bh)


def _sigmoid(x):
    return 1.0 / (1.0 + jnp.exp(-x))


def _silu(x):
    return x * _sigmoid(x)


def _softplus(x):
    return jnp.maximum(x, 0.0) + jnp.log(1.0 + jnp.exp(-jnp.abs(x)))


def _group_sum(x, group_ones):
    cols = []
    for c in range(x.shape[1] // LANES):
        cols.append(_dot_const_rhs(x[:, c * LANES:(c + 1) * LANES], group_ones))
    return cols[0] if len(cols) == 1 else jnp.concatenate(cols, axis=1)


def _group_ones(group):
    i = lax.broadcasted_iota(jnp.int32, (LANES, LANES), 0) // group
    j = lax.broadcasted_iota(jnp.int32, (LANES, LANES), 1) // group
    return (i == j).astype(BF16)


def _shift_rows(x, halo, s, seq_start):
    rolled = pltpu.roll(x, s, 0)
    hr = pltpu.roll(halo, s, 0)
    hr = jnp.where(seq_start, 0.0, hr)
    rows = lax.broadcasted_iota(jnp.int32, hr.shape, 0)
    head = jnp.where(rows < s, hr, rolled[:SUBLANES])
    return jnp.concatenate([head, rolled[SUBLANES:]], axis=0)


def _rmsnorm_kernel(x_ref, g_ref, o_ref):
    x = x_ref[...]
    ms = jnp.mean(x * x, axis=-1, keepdims=True)
    o_ref[...] = (x * lax.rsqrt(ms + NORM_EPS) * g_ref[...]).astype(o_ref.dtype)


def rmsnorm(x, g, out_dtype, tm=512):
    t, d = x.shape
    tm = min(tm, t)
    return pl.pallas_call(
        _rmsnorm_kernel,
        grid=(t // tm,),
        in_specs=[pl.BlockSpec((tm, d), lambda i: (i, 0)),
                  pl.BlockSpec((1, d), lambda i: (0, 0))],
        out_specs=pl.BlockSpec((tm, d), lambda i: (i, 0)),
        out_shape=jax.ShapeDtypeStruct((t, d), out_dtype),
        name="rmsnorm",
        compiler_params=_cparams(("parallel",)),
    )(x, g.reshape(1, d))


def _matmul_kernel(a_ref, w_ref, o_ref):
    o_ref[...] = _dot(a_ref[...], w_ref[...]).astype(o_ref.dtype)


def matmul(a, w, tm, tn, out_dtype=F32):
    m, k = a.shape
    n = w.shape[1]
    tm = min(tm, m)
    return pl.pallas_call(
        _matmul_kernel,
        grid=(m // tm, n // tn),
        in_specs=[pl.BlockSpec((tm, k), lambda i, j: (i, 0)),
                  pl.BlockSpec((k, tn), lambda i, j: (0, j))],
        out_specs=pl.BlockSpec((tm, tn), lambda i, j: (i, j)),
        out_shape=jax.ShapeDtypeStruct((m, n), out_dtype),
        name="inproj",
        compiler_params=_cparams(("parallel", "parallel")),
    )(a, w)


def _outproj_kernel(y_ref, ym_ref, w1_ref, w2_ref, x_ref, o_ref):
    acc = _dot(y_ref[...], w1_ref[...]) + _dot(ym_ref[...], w2_ref[...])
    o_ref[...] = x_ref[...] + acc


def outproj(y, ym, w1, w2, x, tm=512, tn=1024):
    m, k1 = y.shape
    k2 = ym.shape[1]
    n = w1.shape[1]
    tm = min(tm, m)
    tn = min(tn, n)
    return pl.pallas_call(
        _outproj_kernel,
        grid=(m // tm, n // tn),
        in_specs=[pl.BlockSpec((tm, k1), lambda i, j: (i, 0)),
                  pl.BlockSpec((tm, k2), lambda i, j: (i, 0)),
                  pl.BlockSpec((k1, tn), lambda i, j: (0, j)),
                  pl.BlockSpec((k2, tn), lambda i, j: (0, j)),
                  pl.BlockSpec((tm, tn), lambda i, j: (i, j))],
        out_specs=pl.BlockSpec((tm, tn), lambda i, j: (i, j)),
        out_shape=jax.ShapeDtypeStruct((m, n), F32),
        name="outproj",
        compiler_params=_cparams(("parallel", "parallel")),
    )(y, ym, w1, w2, x)


def _ffn_kernel(h_ref, hh_ref, wg_ref, wv_ref, cg_ref, cv_ref, wd_ref, x_ref, o_ref, hcat_ref,
                *, tm, seq, halo):
    i = pl.program_id(0)
    j = pl.program_id(1)

    @pl.when(j == 0)
    def _():
        hh = hh_ref[...]
        hcat_ref[0:halo, :] = jnp.where((i * tm) % seq == 0, jnp.zeros_like(hh), hh)
        hcat_ref[halo:, :] = h_ref[...]
        o_ref[...] = x_ref[...]

    hc = hcat_ref[...]

    def branch(w_ref, c_ref):
        u = _dot(hc, w_ref[...])
        c = c_ref[...]
        return (c[0:1] * pltpu.roll(u, 2, 0)[halo:]
                + c[1:2] * pltpu.roll(u, 1, 0)[halo:]
                + c[2:3] * u[halo:])

    act = _silu(branch(wg_ref, cg_ref)) * branch(wv_ref, cv_ref)
    o_ref[...] += _dot(act.astype(BF16), wd_ref[...])


def ffn(h, w_up, conv, w_down, x, seq, tm=512, tf=512):
    t, d = h.shape
    f = w_down.shape[0]
    tm = min(tm, seq)
    tf = min(tf, f)
    nf = f // tf
    hb = 2 * SUBLANES
    kern = functools.partial(_ffn_kernel, tm=tm, seq=seq, halo=hb)
    return pl.pallas_call(
        kern,
        grid=(t // tm, nf),
        in_specs=[
            pl.BlockSpec((tm, d), lambda i, j: (i, 0)),
            pl.BlockSpec((hb, d), lambda i, j: (jnp.maximum(i * (tm // hb) - 1, 0), 0)),
            pl.BlockSpec((d, tf), lambda i, j: (0, j)),
            pl.BlockSpec((d, tf), lambda i, j: (0, nf + j)),
            pl.BlockSpec((FFN_CONV, tf), lambda i, j: (0, j)),
            pl.BlockSpec((FFN_CONV, tf), lambda i, j: (0, nf + j)),
            pl.BlockSpec((tf, d), lambda i, j: (j, 0)),
            pl.BlockSpec((tm, d), lambda i, j: (i, 0)),
        ],
        out_specs=pl.BlockSpec((tm, d), lambda i, j: (i, 0)),
        out_shape=jax.ShapeDtypeStruct((t, d), F32),
        scratch_shapes=[pltpu.VMEM((hb + tm, d), BF16)],
        name="ffn",
        compiler_params=_cparams(("parallel", "arbitrary")),
    )(h, h, w_up, w_up, conv, conv, w_down, x)


def _swa_kernel(sink_ref, q_ref, kp_ref, kc_ref, vp_ref, vc_ref, o_ref, *, blk):
    n = pl.program_id(1)
    dh = SWA_HEAD_DIM
    t = lax.broadcasted_iota(jnp.int32, (blk, 2 * blk), 0)
    j = lax.broadcasted_iota(jnp.int32, (blk, 2 * blk), 1)
    dist = t - j + blk
    valid = (dist >= 0) & (dist < SWA_WINDOW) & ((j >= blk) | (n > 0))
    distf = dist.astype(F32)
    scale = dh ** -0.5
    for kh in range(SWA_KV_HEADS):
        ks = slice(kh * dh, (kh + 1) * dh)
        k = jnp.concatenate([kp_ref[0, :, ks], kc_ref[0, :, ks]], axis=0).astype(BF16)
        v = jnp.concatenate([vp_ref[0, :, ks], vc_ref[0, :, ks]], axis=0).astype(BF16)
        for g in range(SWA_GROUP):
            h = kh * SWA_GROUP + g
            slope = 2.0 ** (-8.0 * (h + 1.0) / SWA_Q_HEADS)
            sink = sink_ref[h]
            q = q_ref[0, :, h * dh:(h + 1) * dh].astype(BF16)
            s = _dot_nt(q, k) * scale - slope * distf
            s = jnp.where(valid, s, -1e30)
            m = jnp.maximum(jnp.max(s, axis=-1, keepdims=True), sink)
            p = jnp.exp(s - m)
            denom = jnp.sum(p, axis=-1, keepdims=True) + jnp.exp(sink - m)
            o = _dot(p.astype(BF16), v) / denom
            o_ref[0, :, h * dh:(h + 1) * dh] = o.astype(o_ref.dtype)


def swa_attention(p3, sinks, q_col, k_col, v_col):
    b, s, _ = p3.shape
    blk = SWA_WINDOW
    kvw = SWA_KV_HEADS * SWA_HEAD_DIM
    kern = functools.partial(_swa_kernel, blk=blk)
    prev = lambda n: jnp.maximum(n - 1, 0)
    return pl.pallas_call(
        kern,
        grid=(b, s // blk),
        in_specs=[
            pl.BlockSpec(memory_space=pltpu.SMEM),
            pl.BlockSpec((1, blk, MIX_WIDTH), lambda bi, n: (bi, n, q_col // MIX_WIDTH)),
            pl.BlockSpec((1, blk, kvw), lambda bi, n: (bi, prev(n), k_col // kvw)),
            pl.BlockSpec((1, blk, kvw), lambda bi, n: (bi, n, k_col // kvw)),
            pl.BlockSpec((1, blk, kvw), lambda bi, n: (bi, prev(n), v_col // kvw)),
            pl.BlockSpec((1, blk, kvw), lambda bi, n: (bi, n, v_col // kvw)),
        ],
        out_specs=pl.BlockSpec((1, blk, MIX_WIDTH), lambda bi, n: (bi, n, 0)),
        out_shape=jax.ShapeDtypeStruct((b, s, MIX_WIDTH), BF16),
        name="swa",
        compiler_params=_cparams(("parallel", "parallel")),
    )(sinks, p3, p3, p3, p3, p3)


def _memattn_kernel(q_ref, kv_ref, o_ref):
    dh = MEM_HEAD_DIM
    scale = dh ** -0.5
    for h in range(MEM_HEADS):
        q = q_ref[0, :, h * dh:(h + 1) * dh].astype(BF16)
        k = kv_ref[0, :, h * dh:(h + 1) * dh].astype(BF16)
        v = kv_ref[0, :, MEM_WIDTH + h * dh:MEM_WIDTH + (h + 1) * dh].astype(BF16)
        s = _dot_nt(q, k) * scale
        m = jnp.max(s, axis=-1, keepdims=True)
        p = jnp.exp(s - m)
        denom = jnp.sum(p, axis=-1, keepdims=True)
        o = _dot(p.astype(BF16), v) / denom
        o_ref[0, :, h * dh:(h + 1) * dh] = o.astype(o_ref.dtype)


def mem_attention(p3, q_col, mem_kv3, tq=512):
    b, s, _ = p3.shape
    m = mem_kv3.shape[1]
    tq = min(tq, s)
    return pl.pallas_call(
        _memattn_kernel,
        grid=(b, s // tq),
        in_specs=[
            pl.BlockSpec((1, tq, MEM_WIDTH), lambda bi, i: (bi, i, q_col // MEM_WIDTH)),
            pl.BlockSpec((1, m, 2 * MEM_WIDTH), lambda bi, i: (bi, 0, 0)),
        ],
        out_specs=pl.BlockSpec((1, tq, MEM_WIDTH), lambda bi, i: (bi, i, 0)),
        out_shape=jax.ShapeDtypeStruct((b, s, MEM_WIDTH), BF16),
        name="memattn",
        compiler_params=_cparams(("parallel", "parallel")),
    )(p3, mem_kv3)


RW_RKV = 0
RW_QMEM = 3 * MIX_WIDTH
RW_LR = RW_QMEM + MEM_WIDTH
RW_LR_WIDTH = 512
RW_N = RW_LR + RW_LR_WIDTH


def _rwkv_pre_kernel(rkv_ref, lr_ref, rkv_h_ref, lr_h_ref, mu_rkv_ref, mu_lr_ref,
                     w0_ref, wdu_ref, a0_ref, wiu_ref, wgu_ref, kk_ref, ka_ref,
                     r_o, k_o, v_o, a_o, b_o, lw_o, g_o, *, tm, seq):
    i = pl.program_id(0)
    seq_start = (i * tm) % seq == 0

    def mixed(x_ref, h_ref, mu_ref):
        x = x_ref[...]
        prev = _shift_rows(x, h_ref[...], 1, seq_start)
        return x + (prev - x) * mu_ref[...]

    rkv = mixed(rkv_ref, rkv_h_ref, mu_rkv_ref)
    lr = mixed(lr_ref, lr_h_ref, mu_lr_ref)
    w = MIX_WIDTH
    r = rkv[:, :w]
    k = rkv[:, w:2 * w]
    v = rkv[:, 2 * w:]
    wd = jnp.tanh(lr[:, 0:LANES]).astype(BF16)
    ad = lr[:, LANES:2 * LANES].astype(BF16)
    gd = _sigmoid(lr[:, 2 * LANES:]).astype(BF16)
    z = w0_ref[...] + _dot(wd, wdu_ref[...])
    w_log = -_softplus(-z) - 0.5
    lw = -jnp.exp(w_log)
    a_sig = _sigmoid(a0_ref[...] + _dot(ad, wiu_ref[...]))
    g = _dot(gd, wgu_ref[...])
    ones64 = _group_ones(RWKV_HEAD_DIM)
    kk = k * kk_ref[...]
    kk = kk * lax.rsqrt(_group_sum(kk * kk, ones64) + 1e-6)
    k2 = k * (1.0 + (a_sig - 1.0) * ka_ref[...])
    r_o[...] = r
    k_o[...] = k2
    v_o[...] = v
    a_o[...] = kk
    b_o[...] = kk * a_sig
    lw_o[...] = lw
    g_o[...] = g


def rwkv_pre(p, mu_rkv, mu_lr, w0, wdu, a0, wiu, wgu, k_k, k_a, seq, tm=256):
    t = p.shape[0]
    tm = min(tm, seq)
    w = MIX_WIDTH
    hb = SUBLANES
    kern = functools.partial(_rwkv_pre_kernel, tm=tm, seq=seq)
    row = lambda i: (i, 0)
    halo = lambda i: jnp.maximum(i * (tm // hb) - 1, 0)
    const = lambda i: (0, 0)
    vec = pl.BlockSpec((1, w), const)
    out = pl.BlockSpec((tm, w), row)
    return pl.pallas_call(
        kern,
        grid=(t // tm,),
        in_specs=[
            pl.BlockSpec((tm, 3 * w), lambda i: (i, 0)),
            pl.BlockSpec((tm, RW_LR_WIDTH), lambda i: (i, RW_LR // RW_LR_WIDTH)),
            pl.BlockSpec((hb, 3 * w), lambda i: (halo(i), 0)),
            pl.BlockSpec((hb, RW_LR_WIDTH), lambda i: (halo(i), RW_LR // RW_LR_WIDTH)),
            pl.BlockSpec((1, 3 * w), const),
            pl.BlockSpec((1, RW_LR_WIDTH), const),
            vec,
            pl.BlockSpec((LANES, w), const),
            vec,
            pl.BlockSpec((LANES, w), const),
            pl.BlockSpec((2 * LANES, w), const),
            vec, vec,
        ],
        out_specs=[out] * 7,
        out_shape=[jax.ShapeDtypeStruct((t, w), F32)] * 7,
        name="rwkv_pre",
        compiler_params=_cparams(("parallel",)),
    )(p, p, p, p, mu_rkv, mu_lr, w0, wdu, a0, wiu, wgu, k_k, k_a)


def _expand(x, lane_a):
    return jnp.concatenate([jnp.where(lane_a, x, 0.0), jnp.where(lane_a, 0.0, x)], axis=0)


def _dot3_stacked(lhs, rhs):
    m = lhs.shape[0]
    lh, ll = _split2(lhs)
    rh, rl = _split2(rhs)
    top = _dot(jnp.concatenate([lh, ll], axis=0), rh)
    return top[:m] + top[m:] + _dot(lh, rl)


def _unit_lower_inverses_wide(xs, lane_a):
    c = xs[0].shape[0]
    rows = lax.broadcasted_iota(jnp.int32, xs[0].shape, 0)
    cols = lax.broadcasted_iota(jnp.int32, xs[0].shape, 1) % c
    eye = jnp.where(rows == cols, 1.0, 0.0)
    ts = [eye + x for x in xs]
    ps = [_dot3_stacked(x, _expand(x, lane_a)) for x in xs]
    for _ in range(int(math.log2(c)) - 2):
        prods = [_dot3_stacked(jnp.concatenate([t, p], axis=0), _expand(p, lane_a))
                 for t, p in zip(ts, ps)]
        ts = [t + pr[:c] for t, pr in zip(ts, prods)]
        ps = [pr[c:] for pr in prods]
    return [t + _dot3_stacked(t, _expand(p, lane_a)) for t, p in zip(ts, ps)]


def _rwkv_scan_kernel(r_ref, k_ref, v_ref, a_ref, b_ref, lw_ref, g_ref,
                      rk_ref, gng_ref, gnb_ref, o_ref, state_ref, *, chunk, npairs):
    c = chunk
    hd = RWKV_HEAD_DIM

    @pl.when(pl.program_id(2) == 0)
    def _():
        state_ref[...] = jnp.zeros_like(state_ref)

    ri = lax.broadcasted_iota(jnp.int32, (c, c), 0)
    ci = lax.broadcasted_iota(jnp.int32, (c, c), 1)
    tri = (ri >= ci).astype(BF16)
    rows = lax.broadcasted_iota(jnp.int32, (c, LANES), 0)
    lanes = lax.broadcasted_iota(jnp.int32, (c, LANES), 1)
    lane_a = lanes < hd
    strict = (lanes % hd) < rows
    incl = (lanes % hd) <= rows
    bi = lax.broadcasted_iota(jnp.int32, (LANES, LANES), 0) // hd
    bj = lax.broadcasted_iota(jnp.int32, (LANES, LANES), 1) // hd
    blockdiag = bi == bj
    ones64 = blockdiag.astype(BF16)

    pairs = range(npairs)
    sls = [slice(p * LANES, (p + 1) * LANES) for p in pairs]
    cat = lambda x, y: jnp.concatenate([x, y], axis=0)
    xp = lambda x: _expand(x, lane_a).astype(BF16)
    r = [r_ref[0, :, sl] for sl in sls]
    k = [k_ref[0, :, sl] for sl in sls]
    v = [v_ref[0, :, sl] for sl in sls]
    a = [a_ref[0, :, sl] for sl in sls]
    b = [b_ref[0, :, sl] for sl in sls]
    lw = [lw_ref[0, :, sl] for sl in sls]
    cs = [_dot_const_lhs(tri, x) for x in lw]
    c_last = [x[c - 1:c, :] for x in cs]
    e_neg = [jnp.exp(-x) for x in cs]
    e_end = [jnp.exp(cl - x) for cl, x in zip(c_last, cs)]
    ar = [cat(a[p] * jnp.exp(cs[p] - lw[p]), r[p] * jnp.exp(cs[p])).astype(BF16) for p in pairs]
    s_k = [_dot_nt(ar[p], xp(k[p] * e_neg[p])) for p in pairs]
    s_b = [_dot_nt(ar[p], xp(b[p] * e_neg[p])) for p in pairs]
    tinv = _unit_lower_inverses_wide([-jnp.where(strict, s[:c], 0.0) for s in s_b], lane_a)
    lm_k = [cat(jnp.where(strict, s[:c], 0.0), jnp.where(incl, s[c:], 0.0)).astype(BF16)
            for s in s_k]
    m_rb = [jnp.where(incl, s[c:], 0.0).astype(BF16) for s in s_b]
    state = [state_ref[p] for p in pairs]
    ah = [_dot_nt(ar[p], state[p].astype(BF16)) for p in pairs]
    lv = [_dot(lm_k[p], xp(v[p])) for p in pairs]
    u = [_dot3_stacked(tinv[p], _expand(ah[p][:c] + lv[p][:c], lane_a)) for p in pairs]
    y = [ah[p][c:] + lv[p][c:] - _dot(m_rb[p], xp(u[p])) for p in pairs]
    upd = [_dot_tn(cat(v[p], -u[p]).astype(BF16),
                   cat(k[p] * e_end[p], b[p] * e_end[p]).astype(BF16)) for p in pairs]
    for p in pairs:
        state_ref[p] = jnp.where(blockdiag, state[p] * jnp.exp(c_last[p]) + upd[p], 0.0)

    mean = [_group_sum(x, ones64) * (1.0 / hd) for x in y]
    yc = [x - m for x, m in zip(y, mean)]
    var = [_group_sum(x * x, ones64) * (1.0 / hd) for x in yc]
    bonus = [_group_sum(r[p] * k[p] * rk_ref[0:1, sls[p]], ones64) * v[p] for p in pairs]
    for p in pairs:
        sl = sls[p]
        yn = yc[p] * lax.rsqrt(var[p] + RWKV_GN_EPS) * gng_ref[0:1, sl] + gnb_ref[0:1, sl]
        o_ref[0, :, sl] = ((yn + bonus[p]) * g_ref[0, :, sl]).astype(o_ref.dtype)


def rwkv_scan(r, k, v, a, b, lw, g, r_k, gn_g, gn_b, batch, seq, lane_block=1536):
    w = MIX_WIDTH
    c = RWKV_CHUNK
    lane_block = min(lane_block, w)
    npairs = lane_block // LANES
    ngroups = w // lane_block
    shp = (batch, seq, w)
    args = [z.reshape(shp) for z in (r, k, v, a, b, lw, g)]
    tok = pl.BlockSpec((1, c, lane_block), lambda bi, gi, n: (bi, n, gi))
    vec = pl.BlockSpec((1, lane_block), lambda bi, gi, n: (0, gi))
    kern = functools.partial(_rwkv_scan_kernel, chunk=c, npairs=npairs)
    return pl.pallas_call(
        kern,
        grid=(batch, ngroups, seq // c),
        in_specs=[tok] * 7 + [vec] * 3,
        out_specs=tok,
        out_shape=jax.ShapeDtypeStruct(shp, BF16),
        scratch_shapes=[pltpu.VMEM((npairs, LANES, LANES), F32)],
        name="rwkv_scan",
        compiler_params=_cparams(("parallel", "parallel", "arbitrary")),
    )(*args, r_k, gn_g, gn_b)


GD_QKV = 0
GD_QKV_WIDTH = 2 * GDN_QK_WIDTH + MIX_WIDTH
GD_Z = GD_QKV_WIDTH
GD_QMEM = GD_Z + MIX_WIDTH
GD_BA = GD_QMEM + MEM_WIDTH
GD_N = 5376


def _gdn_pre_kernel(x_ref, ba_ref, xh_ref, cw_ref, alog_ref, dtb_ref,
                    q_o, k_o, v_o, gb_o, *, tm, seq):
    i = pl.program_id(0)
    seq_start = (i * tm) % seq == 0
    x = x_ref[...]
    halo = xh_ref[...]
    cw = cw_ref[...]
    y = cw[GDN_CONV - 1:GDN_CONV] * x
    for s in range(1, GDN_CONV):
        y = y + cw[GDN_CONV - 1 - s:GDN_CONV - s] * _shift_rows(x, halo, s, seq_start)
    y = _silu(y)
    ones128 = jnp.ones((LANES, LANES), BF16)
    qw = GDN_QK_WIDTH
    q = y[:, :qw]
    k = y[:, qw:2 * qw]
    q = q * lax.rsqrt(_group_sum(q * q, ones128) + 1e-6) * (GDN_HEAD_DIM ** -0.5)
    k = k * lax.rsqrt(_group_sum(k * k, ones128) + 1e-6)
    q_o[...] = q
    k_o[...] = k
    v_o[...] = y[:, 2 * qw:]
    ba = ba_ref[...]
    lanes = lax.broadcasted_iota(jnp.int32, ba.shape, 1)
    beta = _sigmoid(ba)
    g = -jnp.exp(alog_ref[...]) * _softplus(ba + dtb_ref[...])
    gb_o[...] = jnp.where(lanes < GDN_V_HEADS, beta, g)


def gdn_pre(p, conv_w, alog_row, dtb_row, seq, tm=256):
    t = p.shape[0]
    tm = min(tm, seq)
    hb = SUBLANES
    kern = functools.partial(_gdn_pre_kernel, tm=tm, seq=seq)
    row = lambda i: (i, 0)
    const = lambda i: (0, 0)
    halo = lambda i: jnp.maximum(i * (tm // hb) - 1, 0)
    return pl.pallas_call(
        kern,
        grid=(t // tm,),
        in_specs=[
            pl.BlockSpec((tm, GD_QKV_WIDTH), row),
            pl.BlockSpec((tm, LANES), lambda i: (i, GD_BA // LANES)),
            pl.BlockSpec((hb, GD_QKV_WIDTH), lambda i: (halo(i), 0)),
            pl.BlockSpec((GDN_CONV, GD_QKV_WIDTH), const),
            pl.BlockSpec((1, LANES), const),
            pl.BlockSpec((1, LANES), const),
        ],
        out_specs=[pl.BlockSpec((tm, GDN_QK_WIDTH), row),
                   pl.BlockSpec((tm, GDN_QK_WIDTH), row),
                   pl.BlockSpec((tm, MIX_WIDTH), row),
                   pl.BlockSpec((tm, LANES), row)],
        out_shape=[jax.ShapeDtypeStruct((t, GDN_QK_WIDTH), F32),
                   jax.ShapeDtypeStruct((t, GDN_QK_WIDTH), F32),
                   jax.ShapeDtypeStruct((t, MIX_WIDTH), F32),
                   jax.ShapeDtypeStruct((t, LANES), F32)],
        name="gdn_pre",
        compiler_params=_cparams(("parallel",)),
    )(p, p, p, conv_w, alog_row, dtb_row)


def _gdn_scan_kernel(q_ref, k_ref, v_ref, gb_ref, z_ref, ng_ref, o_ref, state_ref, *, chunk):
    c = chunk
    dh = GDN_HEAD_DIM
    nh = GDN_V_HEADS
    npairs = nh // 2

    @pl.when(pl.program_id(1) == 0)
    def _():
        state_ref[...] = jnp.zeros_like(state_ref)

    ri = lax.broadcasted_iota(jnp.int32, (c, c), 0)
    ci = lax.broadcasted_iota(jnp.int32, (c, c), 1)
    tri = (ri >= ci).astype(BF16)
    ones_cc = jnp.ones((c, c), BF16)
    rows = lax.broadcasted_iota(jnp.int32, (c, LANES), 0)
    lanes = lax.broadcasted_iota(jnp.int32, (c, LANES), 1)
    lane_a = lanes < c
    col = lanes % c
    causal = col <= rows
    strict = col < rows
    diag = col == rows
    zeros = jnp.zeros((c, dh), F32)
    cat = lambda x, y: jnp.concatenate([x, y], axis=0)
    wide = lambda xs, i: jnp.where(lane_a, xs[2 * i], xs[2 * i + 1])

    gb = gb_ref[0]
    gcum = _dot_const_lhs(tri, gb)
    bcol = lambda x, j: jnp.broadcast_to(x[:, j:j + 1], (c, dh))
    beta = [bcol(gb, h) for h in range(nh)]
    gc = [bcol(gcum, nh + h) for h in range(nh)]
    e_gc = [jnp.exp(x) for x in gc]
    g_last = [x[c - 1:c, :] for x in gc]
    q = [q_ref[0, :, i * dh:(i + 1) * dh] for i in range(npairs)]
    k = [k_ref[0, :, i * dh:(i + 1) * dh] for i in range(npairs)]
    v = [v_ref[0, :, h * dh:(h + 1) * dh] for h in range(nh)]

    gcs_w = [wide(gc, i) for i in range(npairs)]
    gcr_w = [_dot_const_lhs(ones_cc, jnp.where(diag, x, 0.0)) for x in gcs_w]
    decay_w = [jnp.where(causal, jnp.exp(jnp.minimum(s - r, 0.0)), 0.0)
               for s, r in zip(gcs_w, gcr_w)]
    sc = [_dot_nt(cat(k[i], q[i]).astype(BF16), cat(k[i], k[i]).astype(BF16))
          for i in range(npairs)]
    lmat_w = [jnp.where(strict, sc[i][:c] * wide(beta, i) * decay_w[i], 0.0) for i in range(npairs)]
    tinv = _unit_lower_inverses_wide([-x for x in lmat_w], lane_a)
    a_qk_w = [(sc[i][c:] * decay_w[i]).astype(BF16) for i in range(npairs)]

    def pair_rhs(i):
        ha, hb = 2 * i, 2 * i + 1
        top = jnp.concatenate([v[ha] * beta[ha], zeros, k[i] * (beta[ha] * e_gc[ha]), zeros], axis=1)
        bot = jnp.concatenate([zeros, v[hb] * beta[hb], zeros, k[i] * (beta[hb] * e_gc[hb])], axis=1)
        return cat(top, bot)

    uw = [_dot3_stacked(tinv[i], pair_rhs(i)) for i in range(npairs)]
    u = [uw[h // 2][:, (h % 2) * dh:(h % 2 + 1) * dh] for h in range(nh)]
    w = [uw[h // 2][:, (2 + h % 2) * dh:(3 + h % 2) * dh] for h in range(nh)]
    state = [state_ref[h] for h in range(nh)]
    ws = [_dot(cat(w[h], q[h // 2] * e_gc[h]).astype(BF16), state[h].astype(BF16))
          for h in range(nh)]
    v_new = [u[h] - ws[h][:c] for h in range(nh)]
    vn_bf = [x.astype(BF16) for x in v_new]
    zb = jnp.zeros((c, dh), BF16)
    intra = [_dot(a_qk_w[i],
                  cat(jnp.concatenate([vn_bf[2 * i], zb], axis=1),
                      jnp.concatenate([zb, vn_bf[2 * i + 1]], axis=1)))
             for i in range(npairs)]
    out = [ws[h][c:] + intra[h // 2][:, (h % 2) * dh:(h % 2 + 1) * dh] for h in range(nh)]
    upd = [_dot_tn((k[h // 2] * jnp.exp(g_last[h] - gc[h])).astype(BF16), vn_bf[h])
           for h in range(nh)]
    for h in range(nh):
        state_ref[h] = state[h] * jnp.exp(g_last[h]) + upd[h]
    for h in range(nh):
        ms = jnp.mean(out[h] * out[h], axis=-1, keepdims=True)
        o = out[h] * lax.rsqrt(ms + NORM_EPS) * ng_ref[...]
        o = o * _silu(z_ref[0, :, h * dh:(h + 1) * dh])
        o_ref[0, :, h * dh:(h + 1) * dh] = o.astype(o_ref.dtype)


def gdn_scan(q, k, v, gb, p3, norm_g, batch, seq):
    c = GDN_CHUNK
    qw = GDN_QK_WIDTH
    w = MIX_WIDTH
    kern = functools.partial(_gdn_scan_kernel, chunk=c)
    tokq = pl.BlockSpec((1, c, qw), lambda bi, n: (bi, n, 0))
    tokv = pl.BlockSpec((1, c, w), lambda bi, n: (bi, n, 0))
    return pl.pallas_call(
        kern,
        grid=(batch, seq // c),
        in_specs=[tokq, tokq, tokv,
                  pl.BlockSpec((1, c, LANES), lambda bi, n: (bi, n, 0)),
                  pl.BlockSpec((1, c, w), lambda bi, n: (bi, n, GD_Z // w)),
                  pl.BlockSpec((1, GDN_HEAD_DIM), lambda bi, n: (0, 0))],
        out_specs=tokv,
        out_shape=jax.ShapeDtypeStruct((batch, seq, w), BF16),
        scratch_shapes=[pltpu.VMEM((GDN_V_HEADS, GDN_HEAD_DIM, GDN_HEAD_DIM), F32)],
        name="gdn_scan",
        compiler_params=_cparams(("parallel", "arbitrary")),
    )(q.reshape(batch, seq, qw), k.reshape(batch, seq, qw), v.reshape(batch, seq, w),
      gb.reshape(batch, seq, LANES), p3, norm_g.reshape(1, GDN_HEAD_DIM))


def _pad_cols(w, n):
    return jnp.pad(w, ((0, 0), (0, n - w.shape[1])))


def _pad_rows(w, n):
    return jnp.pad(w, ((0, n - w.shape[0]), (0, 0)))


def _mixer_a(h, w_in, sinks, batch, seq):
    p = matmul(h, w_in.astype(BF16), tm=512, tn=1280)
    p3 = p.reshape(batch, seq, -1)
    y = swa_attention(p3, sinks, q_col=0, k_col=MIX_WIDTH,
                      v_col=MIX_WIDTH + SWA_KV_HEADS * SWA_HEAD_DIM)
    return y, p3, 2048


def _mixer_b(h, w_in, mu, w0, w_decay_up, a0, w_iclr_up, w_gate_up, k_k, k_a, r_k, gn_g, gn_b,
             batch, seq):
    w = MIX_WIDTH
    dr = w_decay_up.shape[0]
    ir = w_iclr_up.shape[0]
    gr = w_gate_up.shape[0]
    o_wd, o_ad, o_gd, o_qm = 3 * w, 3 * w + dr, 3 * w + dr + ir, 3 * w + dr + ir + gr
    w_in_r = jnp.concatenate([
        w_in[:, :3 * w], w_in[:, o_qm:],
        _pad_cols(w_in[:, o_wd:o_ad], LANES), _pad_cols(w_in[:, o_ad:o_gd], LANES),
        w_in[:, o_gd:o_qm]], axis=1).astype(BF16)
    mu_rkv = mu[:3 * w].reshape(1, -1)
    mu_lr = jnp.concatenate([jnp.pad(mu[o_wd:o_ad], (0, LANES - dr)),
                             jnp.pad(mu[o_ad:o_gd], (0, LANES - ir)),
                             mu[o_gd:o_qm]]).reshape(1, -1)
    p = matmul(h, w_in_r, tm=512, tn=512)
    row = lambda z: z.reshape(1, -1)
    r, k, v, a, b, lw, g = rwkv_pre(
        p, mu_rkv, mu_lr, row(w0), _pad_rows(w_decay_up, LANES).astype(BF16), row(a0),
        _pad_rows(w_iclr_up, LANES).astype(BF16), w_gate_up.astype(BF16), row(k_k), row(k_a), seq)
    y = rwkv_scan(r, k, v, a, b, lw, g, row(r_k), row(gn_g), row(gn_b), batch, seq)
    return y, p.reshape(batch, seq, -1), RW_QMEM


def _mixer_c(h, w_in, conv_w, a_log, dt_bias, norm_g, batch, seq):
    w = MIX_WIDTH
    nh = GDN_V_HEADS
    o_z = GD_QKV_WIDTH
    o_bt = o_z + w
    o_at = o_bt + nh
    o_qm = o_at + nh
    ba = w_in[:, o_bt:o_qm]
    w_in_r = jnp.concatenate([
        w_in[:, :o_z], w_in[:, o_z:o_bt], w_in[:, o_qm:], _pad_cols(ba, LANES)], axis=1)
    w_in_r = _pad_cols(w_in_r, GD_N).astype(BF16)
    p = matmul(h, w_in_r, tm=512, tn=768)
    pad = LANES - 2 * nh
    alog_row = jnp.concatenate([jnp.zeros((nh,), F32), a_log, jnp.zeros((pad,), F32)]).reshape(1, -1)
    dtb_row = jnp.concatenate([jnp.zeros((nh,), F32), dt_bias, jnp.zeros((pad,), F32)]).reshape(1, -1)
    q, k, v, gb = gdn_pre(p, conv_w, alog_row, dtb_row, seq)
    p3 = p.reshape(batch, seq, -1)
    y = gdn_scan(q, k, v, gb, p3, norm_g, batch, seq)
    return y, p3, GD_QMEM


def kernel(x, mem, attn_norm, mem_norm, w_mem_kv, w_out, ffn_norm, w_ffn_up, ffn_conv, w_ffn_down,
           final_norm, a_w_in, a_sinks, b_w_in, b_mu, b_w0, b_w_decay_up, b_a0, b_w_iclr_up,
           b_w_gate_up, b_k_k, b_k_a, b_r_k, b_gn_g, b_gn_b, c_w_in, c_conv, c_a_log, c_dt_bias,
           c_norm_g):
    batch, seq, d = x.shape
    mlen = mem.shape[1]
    depth = attn_norm.shape[0]
    x = x.reshape(batch * seq, d)
    mem2 = mem.reshape(batch * mlen, d)
    for i in range(depth):
        kind, j = i % 3, i // 3
        h = rmsnorm(x, attn_norm[i], BF16)
        hm = rmsnorm(mem2, mem_norm[i], BF16)
        mem_kv = matmul(hm, w_mem_kv[i].astype(BF16), tm=512, tn=1024)
        mem_kv3 = mem_kv.reshape(batch, mlen, -1)
        if kind == 0:
            y, p3, q_col = _mixer_a(h, a_w_in[j], a_sinks[j], batch, seq)
        elif kind == 1:
            y, p3, q_col = _mixer_b(h, b_w_in[j], b_mu[j], b_w0[j], b_w_decay_up[j], b_a0[j],
                                    b_w_iclr_up[j], b_w_gate_up[j], b_k_k[j], b_k_a[j],
                                    b_r_k[j].reshape(-1), b_gn_g[j], b_gn_b[j], batch, seq)
        else:
            y, p3, q_col = _mixer_c(h, c_w_in[j], c_conv[j], c_a_log[j], c_dt_bias[j],
                                    c_norm_g[j], batch, seq)
        y_mem = mem_attention(p3, q_col, mem_kv3)
        wo = w_out[i].astype(BF16)
        x = outproj(y.reshape(batch * seq, -1), y_mem.reshape(batch * seq, -1),
                    wo[:MIX_WIDTH], wo[MIX_WIDTH:], x)
        hf = rmsnorm(x, ffn_norm[i], BF16)
        x = ffn(hf, w_ffn_up[i].astype(BF16), ffn_conv[i], w_ffn_down[i].astype(BF16), x, seq)
    out = rmsnorm(x, final_norm, F32)
    return out.reshape(batch, seq, d)
```

```python
import functools
import math

import jax
import jax.numpy as jnp
from jax import lax
from jax.experimental import pallas as pl
from jax.experimental.pallas import tpu as pltpu

F32 = jnp.float32
BF16 = jnp.bfloat16

D_MODEL = 2048
DEPTH = 4
MIX_WIDTH = 1536
MEM_HEADS = 4
MEM_HEAD_DIM = 128
MEM_WIDTH = 512
NORM_EPS = 1e-6
SWA_HEAD_DIM = 64
SWA_Q_HEADS = 24
SWA_KV_HEADS = 4
SWA_GROUP = 6
SWA_WINDOW = 128
SWA_HEADS_PER_STAGE = 12
RWKV_HEAD_DIM = 64
RWKV_GN_EPS = 64e-5
RWKV_CHUNK = 64
GDN_HEAD_DIM = 128
GDN_V_HEADS = 12
GDN_QK_WIDTH = 768
GDN_CONV = 4
GDN_CHUNK = 64
D_FF = 5632
FFN_CONV = 3

LANES = 128
SUBLANES = 8
VMEM_LIMIT_BYTES = 56 * 1024 * 1024


def _cparams(semantics):
    return pltpu.CompilerParams(dimension_semantics=semantics,
                                vmem_limit_bytes=VMEM_LIMIT_BYTES)


def _dot(a, b):
    return jnp.dot(a, b, preferred_element_type=F32)


def _dot_nt(a, b):
    return lax.dot_general(a, b, (((1,), (1,)), ((), ())), preferred_element_type=F32)


def _dot_tn(a, b):
    return lax.dot_general(a, b, (((0,), (0,)), ((), ())), preferred_element_type=F32)


def _split2(x):
    hi = x.astype(BF16)
    lo = (x - hi.astype(F32)).astype(BF16)
    return hi, lo


def _split3(x):
    hi = x.astype(BF16)
    r1 = x - hi.astype(F32)
    mid = r1.astype(BF16)
    lo = (r1 - mid.astype(F32)).astype(BF16)
    return hi, mid, lo


def _dot_const_lhs(c_bf16, x):
    hi, mid, lo = _split3(x)
    return _dot(c_bf16, hi) + _dot(c_bf16, mid) + _dot(c_bf16, lo)


def _dot_const_rhs(x, c_bf16):
    hi, mid, lo = _split3(x)
    return _dot(hi, c_bf16) + _dot(mid, c_bf16) + _dot(lo, c_bf16)


def _sigmoid(x):
    return 1.0 / (1.0 + jnp.exp(-x))


def _silu(x):
    return x * _sigmoid(x)


def _softplus(x):
    return jnp.maximum(x, 0.0) + jnp.log(1.0 + jnp.exp(-jnp.abs(x)))


def _group_sum(x, group_ones):
    cols = []
    for c in range(x.shape[1] // LANES):
        cols.append(_dot_const_rhs(x[:, c * LANES:(c + 1) * LANES], group_ones))
    return cols[0] if len(cols) == 1 else jnp.concatenate(cols, axis=1)


def _group_ones(group):
    i = lax.broadcasted_iota(jnp.int32, (LANES, LANES), 0) // group
    j = lax.broadcasted_iota(jnp.int32, (LANES, LANES), 1) // group
    return (i == j).astype(BF16)


def _shift_rows(x, halo, s, seq_start):
    rolled = pltpu.roll(x, s, 0)
    hr = pltpu.roll(halo, s, 0)
    hr = jnp.where(seq_start, 0.0, hr)
    rows = lax.broadcasted_iota(jnp.int32, hr.shape, 0)
    head = jnp.where(rows < s, hr, rolled[:SUBLANES])
    return jnp.concatenate([head, rolled[SUBLANES:]], axis=0)


def _rms_normalize(x, g):
    ms = jnp.mean(x * x, axis=-1, keepdims=True)
    return x * lax.rsqrt(ms + NORM_EPS) * g


def _rmsnorm_kernel(x_ref, g_ref, o_ref):
    o_ref[...] = _rms_normalize(x_ref[...], g_ref[...]).astype(o_ref.dtype)


def rmsnorm(x, g, out_dtype, tm=512):
    t, d = x.shape
    tm = min(tm, t)
    return pl.pallas_call(
        _rmsnorm_kernel,
        grid=(t // tm,),
        in_specs=[pl.BlockSpec((tm, d), lambda i: (i, 0)),
                  pl.BlockSpec((1, d), lambda i: (0, 0))],
        out_specs=pl.BlockSpec((tm, d), lambda i: (i, 0)),
        out_shape=jax.ShapeDtypeStruct((t, d), out_dtype),
        name="rmsnorm",
        compiler_params=_cparams(("parallel",)),
    )(x, g.reshape(1, d))


def _norm_matmul_kernel(x_ref, g_ref, w_ref, o_ref, xn_ref):
    @pl.when(pl.program_id(1) == 0)
    def _():
        xn_ref[...] = _rms_normalize(x_ref[...], g_ref[...]).astype(BF16)

    o_ref[...] = _dot(xn_ref[...], w_ref[...])


def norm_matmul(x, g, w, tm, tn):
    m, k = x.shape
    n = w.shape[1]
    tm = min(tm, m)
    return pl.pallas_call(
        _norm_matmul_kernel,
        grid=(m // tm, n // tn),
        in_specs=[pl.BlockSpec((tm, k), lambda i, j: (i, 0)),
                  pl.BlockSpec((1, k), lambda i, j: (0, 0)),
                  pl.BlockSpec((k, tn), lambda i, j: (0, j))],
        out_specs=pl.BlockSpec((tm, tn), lambda i, j: (i, j)),
        out_shape=jax.ShapeDtypeStruct((m, n), F32),
        scratch_shapes=[pltpu.VMEM((tm, k), BF16)],
        name="inproj",
        compiler_params=_cparams(("parallel", "arbitrary")),
    )(x, g.reshape(1, k), w)


def _outproj_kernel(y_ref, ym_ref, w1_ref, w2_ref, x_ref, o_ref):
    acc = _dot(y_ref[...], w1_ref[...]) + _dot(ym_ref[...], w2_ref[...])
    o_ref[...] = x_ref[...] + acc


def outproj(y, ym, w1, w2, x, tm=512):
    m, k1 = y.shape
    k2 = ym.shape[1]
    n = w1.shape[1]
    tm = min(tm, m)
    return pl.pallas_call(
        _outproj_kernel,
        grid=(m // tm,),
        in_specs=[pl.BlockSpec((tm, k1), lambda i: (i, 0)),
                  pl.BlockSpec((tm, k2), lambda i: (i, 0)),
                  pl.BlockSpec((k1, n), lambda i: (0, 0)),
                  pl.BlockSpec((k2, n), lambda i: (0, 0)),
                  pl.BlockSpec((tm, n), lambda i: (i, 0))],
        out_specs=pl.BlockSpec((tm, n), lambda i: (i, 0)),
        out_shape=jax.ShapeDtypeStruct((m, n), F32),
        name="outproj",
        compiler_params=_cparams(("parallel",)),
    )(y, ym, w1, w2, x)


def _ffn_kernel(x_ref, xh_ref, g_ref, wg_ref, wv_ref, cg_ref, cv_ref, wd_ref, o_ref, hcat_ref,
                *, tm, seq, halo):
    i = pl.program_id(0)
    j = pl.program_id(1)

    @pl.when(j == 0)
    def _():
        g = g_ref[...]
        hh = _rms_normalize(xh_ref[...], g).astype(BF16)
        hcat_ref[0:halo, :] = jnp.where((i * tm) % seq == 0, jnp.zeros_like(hh), hh)
        x = x_ref[...]
        hcat_ref[halo:, :] = _rms_normalize(x, g).astype(BF16)
        o_ref[...] = x

    hc = hcat_ref[...]

    def branch(w_ref, c_ref):
        u = _dot(hc, w_ref[...])
        c = c_ref[...]
        return (c[0:1] * pltpu.roll(u, 2, 0)[halo:]
                + c[1:2] * pltpu.roll(u, 1, 0)[halo:]
                + c[2:3] * u[halo:])

    act = _silu(branch(wg_ref, cg_ref)) * branch(wv_ref, cv_ref)
    o_ref[...] += _dot(act.astype(BF16), wd_ref[...])


def ffn(x, g, w_up, conv, w_down, seq, tm=512, tf=512):
    t, d = x.shape
    f = w_down.shape[0]
    tm = min(tm, seq)
    tf = min(tf, f)
    nf = f // tf
    hb = 2 * SUBLANES
    kern = functools.partial(_ffn_kernel, tm=tm, seq=seq, halo=hb)
    return pl.pallas_call(
        kern,
        grid=(t // tm, nf),
        in_specs=[
            pl.BlockSpec((tm, d), lambda i, j: (i, 0)),
            pl.BlockSpec((hb, d), lambda i, j: (jnp.maximum(i * (tm // hb) - 1, 0), 0)),
            pl.BlockSpec((1, d), lambda i, j: (0, 0)),
            pl.BlockSpec((d, tf), lambda i, j: (0, j)),
            pl.BlockSpec((d, tf), lambda i, j: (0, nf + j)),
            pl.BlockSpec((FFN_CONV, tf), lambda i, j: (0, j)),
            pl.BlockSpec((FFN_CONV, tf), lambda i, j: (0, nf + j)),
            pl.BlockSpec((tf, d), lambda i, j: (j, 0)),
        ],
        out_specs=pl.BlockSpec((tm, d), lambda i, j: (i, 0)),
        out_shape=jax.ShapeDtypeStruct((t, d), F32),
        scratch_shapes=[pltpu.VMEM((hb + tm, d), BF16)],
        name="ffn",
        compiler_params=_cparams(("parallel", "arbitrary")),
    )(x, x, g.reshape(1, d), w_up, w_up, conv, conv, w_down)


def _swa_kernel(sink_ref, q_ref, kp_ref, kc_ref, vp_ref, vc_ref, o_ref, bias_ref, *, blk):
    n = pl.program_id(1)
    dh = SWA_HEAD_DIM
    masked = -1e30

    @pl.when(jnp.logical_and(pl.program_id(0) == 0, n == 0))
    def _():
        t = lax.broadcasted_iota(jnp.int32, (blk, 2 * blk), 0)
        j = lax.broadcasted_iota(jnp.int32, (blk, 2 * blk), 1)
        dist = t - j + blk
        valid = (dist >= 0) & (dist < SWA_WINDOW)
        distf = dist.astype(F32)
        for h in range(SWA_Q_HEADS):
            slope = 2.0 ** (-8.0 * (h + 1.0) / SWA_Q_HEADS)
            bias = jnp.where(valid, -slope * distf, masked)
            bias_ref[1, h] = bias
            bias_ref[0, h] = jnp.where(j >= blk, bias, masked)

    slot = jnp.minimum(n, 1)
    scale = dh ** -0.5
    kv = []
    for kh in range(SWA_KV_HEADS):
        ks = slice(kh * dh, (kh + 1) * dh)
        kv.append((jnp.concatenate([kp_ref[0, :, ks], kc_ref[0, :, ks]], axis=0).astype(BF16),
                   jnp.concatenate([vp_ref[0, :, ks], vc_ref[0, :, ks]], axis=0).astype(BF16)))
    for h0 in range(0, SWA_Q_HEADS, SWA_HEADS_PER_STAGE):
        hs = list(range(h0, h0 + SWA_HEADS_PER_STAGE))
        sink = [sink_ref[h] for h in hs]
        q = [(q_ref[0, :, h * dh:(h + 1) * dh] * scale).astype(BF16) for h in hs]
        s = [_dot_nt(qi, kv[h // SWA_GROUP][0]) + bias_ref[slot, h] for qi, h in zip(q, hs)]
        m = [jnp.maximum(jnp.max(si, axis=-1, keepdims=True), sk) for si, sk in zip(s, sink)]
        p = [jnp.exp(si - mi) for si, mi in zip(s, m)]
        denom = [jnp.sum(pi, axis=-1, keepdims=True) + jnp.exp(sk - mi)
                 for pi, sk, mi in zip(p, sink, m)]
        o = [_dot(pi.astype(BF16), kv[h // SWA_GROUP][1]) / di for pi, di, h in zip(p, denom, hs)]
        for h, oi in zip(hs, o):
            o_ref[0, :, h * dh:(h + 1) * dh] = oi.astype(o_ref.dtype)


def swa_attention(p3, sinks, q_col, k_col, v_col):
    b, s, _ = p3.shape
    blk = SWA_WINDOW
    kvw = SWA_KV_HEADS * SWA_HEAD_DIM
    kern = functools.partial(_swa_kernel, blk=blk)
    prev = lambda n: jnp.maximum(n - 1, 0)
    return pl.pallas_call(
        kern,
        grid=(b, s // blk),
        in_specs=[
            pl.BlockSpec(memory_space=pltpu.SMEM),
            pl.BlockSpec((1, blk, MIX_WIDTH), lambda bi, n: (bi, n, q_col // MIX_WIDTH)),
            pl.BlockSpec((1, blk, kvw), lambda bi, n: (bi, prev(n), k_col // kvw)),
            pl.BlockSpec((1, blk, kvw), lambda bi, n: (bi, n, k_col // kvw)),
            pl.BlockSpec((1, blk, kvw), lambda bi, n: (bi, prev(n), v_col // kvw)),
            pl.BlockSpec((1, blk, kvw), lambda bi, n: (bi, n, v_col // kvw)),
        ],
        out_specs=pl.BlockSpec((1, blk, MIX_WIDTH), lambda bi, n: (bi, n, 0)),
        out_shape=jax.ShapeDtypeStruct((b, s, MIX_WIDTH), BF16),
        scratch_shapes=[pltpu.VMEM((2, SWA_Q_HEADS, blk, 2 * blk), F32)],
        name="swa",
        compiler_params=_cparams(("arbitrary", "arbitrary")),
    )(sinks, p3, p3, p3, p3, p3)


def _memattn_kernel(q_ref, kv_ref, o_ref):
    dh = MEM_HEAD_DIM
    scale = dh ** -0.5
    for h in range(MEM_HEADS):
        q = q_ref[0, :, h * dh:(h + 1) * dh].astype(BF16)
        k = kv_ref[0, :, h * dh:(h + 1) * dh].astype(BF16)
        v = kv_ref[0, :, MEM_WIDTH + h * dh:MEM_WIDTH + (h + 1) * dh].astype(BF16)
        s = _dot_nt(q, k) * scale
        m = jnp.max(s, axis=-1, keepdims=True)
        p = jnp.exp(s - m)
        denom = jnp.sum(p, axis=-1, keepdims=True)
        o = _dot(p.astype(BF16), v) / denom
        o_ref[0, :, h * dh:(h + 1) * dh] = o.astype(o_ref.dtype)


def mem_attention(p3, q_col, mem_kv3, tq=512):
    b, s, _ = p3.shape
    m = mem_kv3.shape[1]
    tq = min(tq, s)
    return pl.pallas_call(
        _memattn_kernel,
        grid=(b, s // tq),
        in_specs=[
            pl.BlockSpec((1, tq, MEM_WIDTH), lambda bi, i: (bi, i, q_col // MEM_WIDTH)),
            pl.BlockSpec((1, m, 2 * MEM_WIDTH), lambda bi, i: (bi, 0, 0)),
        ],
        out_specs=pl.BlockSpec((1, tq, MEM_WIDTH), lambda bi, i: (bi, i, 0)),
        out_shape=jax.ShapeDtypeStruct((b, s, MEM_WIDTH), BF16),
        name="memattn",
        compiler_params=_cparams(("parallel", "parallel")),
    )(p3, mem_kv3)


RW_RKV = 0
RW_QMEM = 3 * MIX_WIDTH
RW_LR = RW_QMEM + MEM_WIDTH
RW_LR_WIDTH = 512
RW_N = RW_LR + RW_LR_WIDTH


def _rwkv_pre_kernel(rkv_ref, lr_ref, rkv_h_ref, lr_h_ref, mu_rkv_ref, mu_lr_ref,
                     w0_ref, wdu_ref, a0_ref, wiu_ref, wgu_ref, kk_ref, ka_ref,
                     r_o, k_o, v_o, a_o, b_o, lw_o, g_o, *, tm, seq):
    i = pl.program_id(0)
    seq_start = (i * tm) % seq == 0

    def mixed(x_ref, h_ref, mu_ref):
        x = x_ref[...]
        prev = _shift_rows(x, h_ref[...], 1, seq_start)
        return x + (prev - x) * mu_ref[...]

    rkv = mixed(rkv_ref, rkv_h_ref, mu_rkv_ref)
    lr = mixed(lr_ref, lr_h_ref, mu_lr_ref)
    w = MIX_WIDTH
    r = rkv[:, :w]
    k = rkv[:, w:2 * w]
    v = rkv[:, 2 * w:]
    wd = jnp.tanh(lr[:, 0:LANES]).astype(BF16)
    ad = lr[:, LANES:2 * LANES].astype(BF16)
    gd = _sigmoid(lr[:, 2 * LANES:]).astype(BF16)
    z = w0_ref[...] + _dot(wd, wdu_ref[...])
    w_log = -_softplus(-z) - 0.5
    lw = -jnp.exp(w_log)
    a_sig = _sigmoid(a0_ref[...] + _dot(ad, wiu_ref[...]))
    g = _dot(gd, wgu_ref[...])
    ones64 = _group_ones(RWKV_HEAD_DIM)
    kk = k * kk_ref[...]
    kk = kk * lax.rsqrt(_group_sum(kk * kk, ones64) + 1e-6)
    k2 = k * (1.0 + (a_sig - 1.0) * ka_ref[...])
    r_o[...] = r
    k_o[...] = k2
    v_o[...] = v
    a_o[...] = kk
    b_o[...] = kk * a_sig
    lw_o[...] = lw
    g_o[...] = g


def rwkv_pre(p, mu_rkv, mu_lr, w0, wdu, a0, wiu, wgu, k_k, k_a, seq, tm=256):
    t = p.shape[0]
    tm = min(tm, seq)
    w = MIX_WIDTH
    hb = SUBLANES
    kern = functools.partial(_rwkv_pre_kernel, tm=tm, seq=seq)
    row = lambda i: (i, 0)
    halo = lambda i: jnp.maximum(i * (tm // hb) - 1, 0)
    const = lambda i: (0, 0)
    vec = pl.BlockSpec((1, w), const)
    out = pl.BlockSpec((tm, w), row)
    return pl.pallas_call(
        kern,
        grid=(t // tm,),
        in_specs=[
            pl.BlockSpec((tm, 3 * w), lambda i: (i, 0)),
            pl.BlockSpec((tm, RW_LR_WIDTH), lambda i: (i, RW_LR // RW_LR_WIDTH)),
            pl.BlockSpec((hb, 3 * w), lambda i: (halo(i), 0)),
            pl.BlockSpec((hb, RW_LR_WIDTH), lambda i: (halo(i), RW_LR // RW_LR_WIDTH)),
            pl.BlockSpec((1, 3 * w), const),
            pl.BlockSpec((1, RW_LR_WIDTH), const),
            vec,
            pl.BlockSpec((LANES, w), const),
            vec,
            pl.BlockSpec((LANES, w), const),
            pl.BlockSpec((2 * LANES, w), const),
            vec, vec,
        ],
        out_specs=[out] * 7,
        out_shape=[jax.ShapeDtypeStruct((t, w), F32)] * 7,
        name="rwkv_pre",
        compiler_params=_cparams(("parallel",)),
    )(p, p, p, p, mu_rkv, mu_lr, w0, wdu, a0, wiu, wgu, k_k, k_a)


def _expand(x, lane_a):
    return jnp.concatenate([jnp.where(lane_a, x, 0.0), jnp.where(lane_a, 0.0, x)], axis=0)


def _dot3_stacked(lhs, rhs):
    m = lhs.shape[0]
    lh, ll = _split2(lhs)
    rh, rl = _split2(rhs)
    top = _dot(jnp.concatenate([lh, ll], axis=0), rh)
    return top[:m] + top[m:] + _dot(lh, rl)


def _unit_lower_inverses_wide(xs, lane_a):
    c = xs[0].shape[0]
    rows = lax.broadcasted_iota(jnp.int32, xs[0].shape, 0)
    cols = lax.broadcasted_iota(jnp.int32, xs[0].shape, 1) % c
    eye = jnp.where(rows == cols, 1.0, 0.0)
    ts = [eye + x for x in xs]
    ps = [_dot3_stacked(x, _expand(x, lane_a)) for x in xs]
    for _ in range(int(math.log2(c)) - 2):
        prods = [_dot3_stacked(jnp.concatenate([t, p], axis=0), _expand(p, lane_a))
                 for t, p in zip(ts, ps)]
        ts = [t + pr[:c] for t, pr in zip(ts, prods)]
        ps = [pr[c:] for pr in prods]
    return [t + _dot3_stacked(t, _expand(p, lane_a)) for t, p in zip(ts, ps)]


def _rwkv_scan_kernel(r_ref, k_ref, v_ref, a_ref, b_ref, lw_ref, g_ref,
                      rk_ref, gng_ref, gnb_ref, o_ref, state_ref, *, chunk, npairs):
    c = chunk
    hd = RWKV_HEAD_DIM

    @pl.when(pl.program_id(2) == 0)
    def _():
        state_ref[...] = jnp.zeros_like(state_ref)

    ri = lax.broadcasted_iota(jnp.int32, (c, c), 0)
    ci = lax.broadcasted_iota(jnp.int32, (c, c), 1)
    tri = (ri >= ci).astype(BF16)
    rows = lax.broadcasted_iota(jnp.int32, (c, LANES), 0)
    lanes = lax.broadcasted_iota(jnp.int32, (c, LANES), 1)
    lane_a = lanes < hd
    strict = (lanes % hd) < rows
    incl = (lanes % hd) <= rows
    bi = lax.broadcasted_iota(jnp.int32, (LANES, LANES), 0) // hd
    bj = lax.broadcasted_iota(jnp.int32, (LANES, LANES), 1) // hd
    blockdiag = bi == bj
    ones64 = blockdiag.astype(BF16)

    pairs = range(npairs)
    sls = [slice(p * LANES, (p + 1) * LANES) for p in pairs]
    cat = lambda x, y: jnp.concatenate([x, y], axis=0)
    xp = lambda x: _expand(x, lane_a).astype(BF16)
    r = [r_ref[0, :, sl] for sl in sls]
    k = [k_ref[0, :, sl] for sl in sls]
    v = [v_ref[0, :, sl] for sl in sls]
    a = [a_ref[0, :, sl] for sl in sls]
    b = [b_ref[0, :, sl] for sl in sls]
    lw = [lw_ref[0, :, sl] for sl in sls]
    cs = [_dot_const_lhs(tri, x) for x in lw]
    c_last = [x[c - 1:c, :] for x in cs]
    e_neg = [jnp.exp(-x) for x in cs]
    e_end = [jnp.exp(cl - x) for cl, x in zip(c_last, cs)]
    ar = [cat(a[p] * jnp.exp(cs[p] - lw[p]), r[p] * jnp.exp(cs[p])).astype(BF16) for p in pairs]
    s_k = [_dot_nt(ar[p], xp(k[p] * e_neg[p])) for p in pairs]
    s_b = [_dot_nt(ar[p], xp(b[p] * e_neg[p])) for p in pairs]
    tinv = _unit_lower_inverses_wide([-jnp.where(strict, s[:c], 0.0) for s in s_b], lane_a)
    lm_k = [cat(jnp.where(strict, s[:c], 0.0), jnp.where(incl, s[c:], 0.0)).astype(BF16)
            for s in s_k]
    m_rb = [jnp.where(incl, s[c:], 0.0).astype(BF16) for s in s_b]
    state = [state_ref[p] for p in pairs]
    ah = [_dot_nt(ar[p], state[p].astype(BF16)) for p in pairs]
    lv = [_dot(lm_k[p], xp(v[p])) for p in pairs]
    u = [_dot3_stacked(tinv[p], _expand(ah[p][:c] + lv[p][:c], lane_a)) for p in pairs]
    y = [ah[p][c:] + lv[p][c:] - _dot(m_rb[p], xp(u[p])) for p in pairs]
    upd = [_dot_tn(cat(v[p], -u[p]).astype(BF16),
                   cat(k[p] * e_end[p], b[p] * e_end[p]).astype(BF16)) for p in pairs]
    for p in pairs:
        state_ref[p] = jnp.where(blockdiag, state[p] * jnp.exp(c_last[p]) + upd[p], 0.0)

    mean = [_group_sum(x, ones64) * (1.0 / hd) for x in y]
    yc = [x - m for x, m in zip(y, mean)]
    var = [_group_sum(x * x, ones64) * (1.0 / hd) for x in yc]
    bonus = [_group_sum(r[p] * k[p] * rk_ref[0:1, sls[p]], ones64) * v[p] for p in pairs]
    for p in pairs:
        sl = sls[p]
        yn = yc[p] * lax.rsqrt(var[p] + RWKV_GN_EPS) * gng_ref[0:1, sl] + gnb_ref[0:1, sl]
        o_ref[0, :, sl] = ((yn + bonus[p]) * g_ref[0, :, sl]).astype(o_ref.dtype)


def rwkv_scan(r, k, v, a, b, lw, g, r_k, gn_g, gn_b, batch, seq, lane_block=1536):
    w = MIX_WIDTH
    c = RWKV_CHUNK
    lane_block = min(lane_block, w)
    npairs = lane_block // LANES
    ngroups = w // lane_block
    shp = (batch, seq, w)
    args = [z.reshape(shp) for z in (r, k, v, a, b, lw, g)]
    tok = pl.BlockSpec((1, c, lane_block), lambda bi, gi, n: (bi, n, gi))
    vec = pl.BlockSpec((1, lane_block), lambda bi, gi, n: (0, gi))
    kern = functools.partial(_rwkv_scan_kernel, chunk=c, npairs=npairs)
    return pl.pallas_call(
        kern,
        grid=(batch, ngroups, seq // c),
        in_specs=[tok] * 7 + [vec] * 3,
        out_specs=tok,
        out_shape=jax.ShapeDtypeStruct(shp, BF16),
        scratch_shapes=[pltpu.VMEM((npairs, LANES, LANES), F32)],
        name="rwkv_scan",
        compiler_params=_cparams(("parallel", "parallel", "arbitrary")),
    )(*args, r_k, gn_g, gn_b)


GD_QKV = 0
GD_QKV_WIDTH = 2 * GDN_QK_WIDTH + MIX_WIDTH
GD_Z = GD_QKV_WIDTH
GD_QMEM = GD_Z + MIX_WIDTH
GD_BA = GD_QMEM + MEM_WIDTH
GD_N = 5376


def _gdn_pre_kernel(x_ref, ba_ref, xh_ref, cw_ref, alog_ref, dtb_ref,
                    q_o, k_o, v_o, gb_o, *, tm, seq):
    i = pl.program_id(0)
    seq_start = (i * tm) % seq == 0
    x = x_ref[...]
    halo = xh_ref[...]
    cw = cw_ref[...]
    y = cw[GDN_CONV - 1:GDN_CONV] * x
    for s in range(1, GDN_CONV):
        y = y + cw[GDN_CONV - 1 - s:GDN_CONV - s] * _shift_rows(x, halo, s, seq_start)
    y = _silu(y)
    ones128 = jnp.ones((LANES, LANES), BF16)
    qw = GDN_QK_WIDTH
    q = y[:, :qw]
    k = y[:, qw:2 * qw]
    q = q * lax.rsqrt(_group_sum(q * q, ones128) + 1e-6) * (GDN_HEAD_DIM ** -0.5)
    k = k * lax.rsqrt(_group_sum(k * k, ones128) + 1e-6)
    q_o[...] = q
    k_o[...] = k
    v_o[...] = y[:, 2 * qw:]
    ba = ba_ref[...]
    lanes = lax.broadcasted_iota(jnp.int32, ba.shape, 1)
    beta = _sigmoid(ba)
    g = -jnp.exp(alog_ref[...]) * _softplus(ba + dtb_ref[...])
    gb_o[...] = jnp.where(lanes < GDN_V_HEADS, beta, g)


def gdn_pre(p, conv_w, alog_row, dtb_row, seq, tm=256):
    t = p.shape[0]
    tm = min(tm, seq)
    hb = SUBLANES
    kern = functools.partial(_gdn_pre_kernel, tm=tm, seq=seq)
    row = lambda i: (i, 0)
    const = lambda i: (0, 0)
    halo = lambda i: jnp.maximum(i * (tm // hb) - 1, 0)
    return pl.pallas_call(
        kern,
        grid=(t // tm,),
        in_specs=[
            pl.BlockSpec((tm, GD_QKV_WIDTH), row),
            pl.BlockSpec((tm, LANES), lambda i: (i, GD_BA // LANES)),
            pl.BlockSpec((hb, GD_QKV_WIDTH), lambda i: (halo(i), 0)),
            pl.BlockSpec((GDN_CONV, GD_QKV_WIDTH), const),
            pl.BlockSpec((1, LANES), const),
            pl.BlockSpec((1, LANES), const),
        ],
        out_specs=[pl.BlockSpec((tm, GDN_QK_WIDTH), row),
                   pl.BlockSpec((tm, GDN_QK_WIDTH), row),
                   pl.BlockSpec((tm, MIX_WIDTH), row),
                   pl.BlockSpec((tm, LANES), row)],
        out_shape=[jax.ShapeDtypeStruct((t, GDN_QK_WIDTH), F32),
                   jax.ShapeDtypeStruct((t, GDN_QK_WIDTH), F32),
                   jax.ShapeDtypeStruct((t, MIX_WIDTH), F32),
                   jax.ShapeDtypeStruct((t, LANES), F32)],
        name="gdn_pre",
        compiler_params=_cparams(("parallel",)),
    )(p, p, p, conv_w, alog_row, dtb_row)


def _gdn_scan_kernel(q_ref, k_ref, v_ref, gb_ref, z_ref, ng_ref, o_ref, state_ref, *, chunk):
    c = chunk
    dh = GDN_HEAD_DIM
    nh = GDN_V_HEADS
    npairs = nh // 2

    @pl.when(pl.program_id(1) == 0)
    def _():
        state_ref[...] = jnp.zeros_like(state_ref)

    ri = lax.broadcasted_iota(jnp.int32, (c, c), 0)
    ci = lax.broadcasted_iota(jnp.int32, (c, c), 1)
    tri = (ri >= ci).astype(BF16)
    ones_cc = jnp.ones((c, c), BF16)
    rows = lax.broadcasted_iota(jnp.int32, (c, LANES), 0)
    lanes = lax.broadcasted_iota(jnp.int32, (c, LANES), 1)
    lane_a = lanes < c
    col = lanes % c
    causal = col <= rows
    strict = col < rows
    diag = col == rows
    zeros = jnp.zeros((c, dh), F32)
    cat = lambda x, y: jnp.concatenate([x, y], axis=0)
    wide = lambda xs, i: jnp.where(lane_a, xs[2 * i], xs[2 * i + 1])

    gb = gb_ref[0]
    gcum = _dot_const_lhs(tri, gb)
    bcol = lambda x, j: jnp.broadcast_to(x[:, j:j + 1], (c, dh))
    beta = [bcol(gb, h) for h in range(nh)]
    gc = [bcol(gcum, nh + h) for h in range(nh)]
    e_gc = [jnp.exp(x) for x in gc]
    g_last = [x[c - 1:c, :] for x in gc]
    q = [q_ref[0, :, i * dh:(i + 1) * dh] for i in range(npairs)]
    k = [k_ref[0, :, i * dh:(i + 1) * dh] for i in range(npairs)]
    v = [v_ref[0, :, h * dh:(h + 1) * dh] for h in range(nh)]

    gcs_w = [wide(gc, i) for i in range(npairs)]
    gcr_w = [_dot_const_lhs(ones_cc, jnp.where(diag, x, 0.0)) for x in gcs_w]
    decay_w = [jnp.where(causal, jnp.exp(jnp.minimum(s - r, 0.0)), 0.0)
               for s, r in zip(gcs_w, gcr_w)]
    sc = [_dot_nt(cat(k[i], q[i]).astype(BF16), cat(k[i], k[i]).astype(BF16))
          for i in range(npairs)]
    lmat_w = [jnp.where(strict, sc[i][:c] * wide(beta, i) * decay_w[i], 0.0) for i in range(npairs)]
    tinv = _unit_lower_inverses_wide([-x for x in lmat_w], lane_a)
    a_qk_w = [(sc[i][c:] * decay_w[i]).astype(BF16) for i in range(npairs)]

    def pair_rhs(i):
        ha, hb = 2 * i, 2 * i + 1
        top = jnp.concatenate([v[ha] * beta[ha], zeros, k[i] * (beta[ha] * e_gc[ha]), zeros], axis=1)
        bot = jnp.concatenate([zeros, v[hb] * beta[hb], zeros, k[i] * (beta[hb] * e_gc[hb])], axis=1)
        return cat(top, bot)

    uw = [_dot3_stacked(tinv[i], pair_rhs(i)) for i in range(npairs)]
    u = [uw[h // 2][:, (h % 2) * dh:(h % 2 + 1) * dh] for h in range(nh)]
    w = [uw[h // 2][:, (2 + h % 2) * dh:(3 + h % 2) * dh] for h in range(nh)]
    state = [state_ref[h] for h in range(nh)]
    ws = [_dot(cat(w[h], q[h // 2] * e_gc[h]).astype(BF16), state[h].astype(BF16))
          for h in range(nh)]
    v_new = [u[h] - ws[h][:c] for h in range(nh)]
    vn_bf = [x.astype(BF16) for x in v_new]
    zb = jnp.zeros((c, dh), BF16)
    intra = [_dot(a_qk_w[i],
                  cat(jnp.concatenate([vn_bf[2 * i], zb], axis=1),
                      jnp.concatenate([zb, vn_bf[2 * i + 1]], axis=1)))
             for i in range(npairs)]
    out = [ws[h][c:] + intra[h // 2][:, (h % 2) * dh:(h % 2 + 1) * dh] for h in range(nh)]
    upd = [_dot_tn((k[h // 2] * jnp.exp(g_last[h] - gc[h])).astype(BF16), vn_bf[h])
           for h in range(nh)]
    for h in range(nh):
        state_ref[h] = state[h] * jnp.exp(g_last[h]) + upd[h]
    for h in range(nh):
        ms = jnp.mean(out[h] * out[h], axis=-1, keepdims=True)
        o = out[h] * lax.rsqrt(ms + NORM_EPS) * ng_ref[...]
        o = o * _silu(z_ref[0, :, h * dh:(h + 1) * dh])
        o_ref[0, :, h * dh:(h + 1) * dh] = o.astype(o_ref.dtype)


def gdn_scan(q, k, v, gb, p3, norm_g, batch, seq):
    c = GDN_CHUNK
    qw = GDN_QK_WIDTH
    w = MIX_WIDTH
    kern = functools.partial(_gdn_scan_kernel, chunk=c)
    tokq = pl.BlockSpec((1, c, qw), lambda bi, n: (bi, n, 0))
    tokv = pl.BlockSpec((1, c, w), lambda bi, n: (bi, n, 0))
    return pl.pallas_call(
        kern,
        grid=(batch, seq // c),
        in_specs=[tokq, tokq, tokv,
                  pl.BlockSpec((1, c, LANES), lambda bi, n: (bi, n, 0)),
                  pl.BlockSpec((1, c, w), lambda bi, n: (bi, n, GD_Z // w)),
                  pl.BlockSpec((1, GDN_HEAD_DIM), lambda bi, n: (0, 0))],
        out_specs=tokv,
        out_shape=jax.ShapeDtypeStruct((batch, seq, w), BF16),
        scratch_shapes=[pltpu.VMEM((GDN_V_HEADS, GDN_HEAD_DIM, GDN_HEAD_DIM), F32)],
        name="gdn_scan",
        compiler_params=_cparams(("parallel", "arbitrary")),
    )(q.reshape(batch, seq, qw), k.reshape(batch, seq, qw), v.reshape(batch, seq, w),
      gb.reshape(batch, seq, LANES), p3, norm_g.reshape(1, GDN_HEAD_DIM))


def _pad_cols(w, n):
    return jnp.pad(w, ((0, 0), (0, n - w.shape[1])))


def _pad_rows(w, n):
    return jnp.pad(w, ((0, n - w.shape[0]), (0, 0)))


def _mixer_a(x, g, w_in, sinks, batch, seq):
    p = norm_matmul(x, g, w_in.astype(BF16), tm=1024, tn=512)
    p3 = p.reshape(batch, seq, -1)
    y = swa_attention(p3, sinks, q_col=0, k_col=MIX_WIDTH,
                      v_col=MIX_WIDTH + SWA_KV_HEADS * SWA_HEAD_DIM)
    return y, p3, 2048


def _mixer_b(x, g, w_in, mu, w0, w_decay_up, a0, w_iclr_up, w_gate_up, k_k, k_a, r_k, gn_g, gn_b,
             batch, seq):
    w = MIX_WIDTH
    dr = w_decay_up.shape[0]
    ir = w_iclr_up.shape[0]
    gr = w_gate_up.shape[0]
    o_wd, o_ad, o_gd, o_qm = 3 * w, 3 * w + dr, 3 * w + dr + ir, 3 * w + dr + ir + gr
    w_in_r = jnp.concatenate([
        w_in[:, :3 * w], w_in[:, o_qm:],
        _pad_cols(w_in[:, o_wd:o_ad], LANES), _pad_cols(w_in[:, o_ad:o_gd], LANES),
        w_in[:, o_gd:o_qm]], axis=1).astype(BF16)
    mu_rkv = mu[:3 * w].reshape(1, -1)
    mu_lr = jnp.concatenate([jnp.pad(mu[o_wd:o_ad], (0, LANES - dr)),
                             jnp.pad(mu[o_ad:o_gd], (0, LANES - ir)),
                             mu[o_gd:o_qm]]).reshape(1, -1)
    p = norm_matmul(x, g, w_in_r, tm=1024, tn=512)
    row = lambda z: z.reshape(1, -1)
    r, k, v, a, b, lw, gate = rwkv_pre(
        p, mu_rkv, mu_lr, row(w0), _pad_rows(w_decay_up, LANES).astype(BF16), row(a0),
        _pad_rows(w_iclr_up, LANES).astype(BF16), w_gate_up.astype(BF16), row(k_k), row(k_a), seq)
    y = rwkv_scan(r, k, v, a, b, lw, gate, row(r_k), row(gn_g), row(gn_b), batch, seq)
    return y, p.reshape(batch, seq, -1), RW_QMEM


def _mixer_c(x, g, w_in, conv_w, a_log, dt_bias, norm_g, batch, seq):
    w = MIX_WIDTH
    nh = GDN_V_HEADS
    o_z = GD_QKV_WIDTH
    o_bt = o_z + w
    o_at = o_bt + nh
    o_qm = o_at + nh
    ba = w_in[:, o_bt:o_qm]
    w_in_r = jnp.concatenate([
        w_in[:, :o_z], w_in[:, o_z:o_bt], w_in[:, o_qm:], _pad_cols(ba, LANES)], axis=1)
    w_in_r = _pad_cols(w_in_r, GD_N).astype(BF16)
    p = norm_matmul(x, g, w_in_r, tm=1024, tn=768)
    pad = LANES - 2 * nh
    alog_row = jnp.concatenate([jnp.zeros((nh,), F32), a_log, jnp.zeros((pad,), F32)]).reshape(1, -1)
    dtb_row = jnp.concatenate([jnp.zeros((nh,), F32), dt_bias, jnp.zeros((pad,), F32)]).reshape(1, -1)
    q, k, v, gb = gdn_pre(p, conv_w, alog_row, dtb_row, seq)
    p3 = p.reshape(batch, seq, -1)
    y = gdn_scan(q, k, v, gb, p3, norm_g, batch, seq)
    return y, p3, GD_QMEM


def kernel(x, mem, attn_norm, mem_norm, w_mem_kv, w_out, ffn_norm, w_ffn_up, ffn_conv, w_ffn_down,
           final_norm, a_w_in, a_sinks, b_w_in, b_mu, b_w0, b_w_decay_up, b_a0, b_w_iclr_up,
           b_w_gate_up, b_k_k, b_k_a, b_r_k, b_gn_g, b_gn_b, c_w_in, c_conv, c_a_log, c_dt_bias,
           c_norm_g):
    batch, seq, d = x.shape
    mlen = mem.shape[1]
    depth = attn_norm.shape[0]
    x = x.reshape(batch * seq, d)
    mem2 = mem.reshape(batch * mlen, d)
    for i in range(depth):
        kind, j = i % 3, i // 3
        g = attn_norm[i]
        mem_kv = norm_matmul(mem2, mem_norm[i], w_mem_kv[i].astype(BF16), tm=1024, tn=512)
        mem_kv3 = mem_kv.reshape(batch, mlen, -1)
        if kind == 0:
            y, p3, q_col = _mixer_a(x, g, a_w_in[j], a_sinks[j], batch, seq)
        elif kind == 1:
            y, p3, q_col = _mixer_b(x, g, b_w_in[j], b_mu[j], b_w0[j], b_w_decay_up[j], b_a0[j],
                                    b_w_iclr_up[j], b_w_gate_up[j], b_k_k[j], b_k_a[j],
                                    b_r_k[j].reshape(-1), b_gn_g[j], b_gn_b[j], batch, seq)
        else:
            y, p3, q_col = _mixer_c(x, g, c_w_in[j], c_conv[j], c_a_log[j], c_dt_bias[j],
                                    c_norm_g[j], batch, seq)
        y_mem = mem_attention(p3, q_col, mem_kv3)
        wo = w_out[i].astype(BF16)
        x = outproj(y.reshape(batch * seq, -1), y_mem.reshape(batch * seq, -1),
                    wo[:MIX_WIDTH], wo[MIX_WIDTH:], x)
        x = ffn(x, ffn_norm[i], w_ffn_up[i].astype(BF16), ffn_conv[i], w_ffn_down[i].astype(BF16), seq)
    out = rmsnorm(x, final_norm, F32)
    return out.reshape(batch, seq, d)
```

```python
import functools
import math

import jax
import jax.numpy as jnp
from jax import lax
from jax.experimental import pallas as pl
from jax.experimental.pallas import tpu as pltpu

F32 = jnp.float32
BF16 = jnp.bfloat16

D_MODEL = 2048
DEPTH = 4
MIX_WIDTH = 1536
MEM_HEADS = 4
MEM_HEAD_DIM = 128
MEM_WIDTH = 512
NORM_EPS = 1e-6
SWA_HEAD_DIM = 64
SWA_Q_HEADS = 24
SWA_KV_HEADS = 4
SWA_GROUP = 6
SWA_WINDOW = 128
SWA_HEADS_PER_STAGE = 12
RWKV_HEAD_DIM = 64
RWKV_GN_EPS = 64e-5
RWKV_CHUNK = 64
GDN_HEAD_DIM = 128
GDN_V_HEADS = 12
GDN_QK_WIDTH = 768
GDN_CONV = 4
GDN_CHUNK = 64
D_FF = 5632
FFN_CONV = 3

LANES = 128
SUBLANES = 8
VMEM_LIMIT_BYTES = 56 * 1024 * 1024


def _cparams(semantics):
    return pltpu.CompilerParams(dimension_semantics=semantics,
                                vmem_limit_bytes=VMEM_LIMIT_BYTES)


def _dot(a, b):
    return jnp.dot(a, b, preferred_element_type=F32)


def _dot_nt(a, b):
    return lax.dot_general(a, b, (((1,), (1,)), ((), ())), preferred_element_type=F32)


def _dot_tn(a, b):
    return lax.dot_general(a, b, (((0,), (0,)), ((), ())), preferred_element_type=F32)


def _split2(x):
    hi = x.astype(BF16)
    lo = (x - hi.astype(F32)).astype(BF16)
    return hi, lo


def _split3(x):
    hi = x.astype(BF16)
    r1 = x - hi.astype(F32)
    mid = r1.astype(BF16)
    lo = (r1 - mid.astype(F32)).astype(BF16)
    return hi, mid, lo


def _dot_const_lhs(c_bf16, x):
    hi, mid, lo = _split3(x)
    return _dot(c_bf16, hi) + _dot(c_bf16, mid) + _dot(c_bf16, lo)


def _sigmoid(x):
    return 1.0 / (1.0 + jnp.exp(-x))


def _silu(x):
    return x * _sigmoid(x)


def _softplus(x):
    return jnp.maximum(x, 0.0) + jnp.log(1.0 + jnp.exp(-jnp.abs(x)))


def _group_sums(xs, group_ones):
    pieces = []
    for x in xs:
        for c in range(x.shape[1] // LANES):
            pieces.extend(_split3(x[:, c * LANES:(c + 1) * LANES]))
    prod = _dot(jnp.concatenate(pieces, axis=0), group_ones)
    outs, row = [], 0
    for x in xs:
        r = x.shape[0]
        cols = []
        for _ in range(x.shape[1] // LANES):
            cols.append(prod[row:row + r] + prod[row + r:row + 2 * r] + prod[row + 2 * r:row + 3 * r])
            row += 3 * r
        outs.append(cols[0] if len(cols) == 1 else jnp.concatenate(cols, axis=1))
    return outs


def _group_sum(x, group_ones):
    return _group_sums([x], group_ones)[0]


def _group_ones(group):
    i = lax.broadcasted_iota(jnp.int32, (LANES, LANES), 0) // group
    j = lax.broadcasted_iota(jnp.int32, (LANES, LANES), 1) // group
    return (i == j).astype(BF16)


def _shift_rows(x, halo, s, seq_start):
    rolled = pltpu.roll(x, s, 0)
    hr = pltpu.roll(halo, s, 0)
    hr = jnp.where(seq_start, 0.0, hr)
    rows = lax.broadcasted_iota(jnp.int32, hr.shape, 0)
    head = jnp.where(rows < s, hr, rolled[:SUBLANES])
    return jnp.concatenate([head, rolled[SUBLANES:]], axis=0)


def _rms_normalize(x, g):
    ms = jnp.mean(x * x, axis=-1, keepdims=True)
    return x * lax.rsqrt(ms + NORM_EPS) * g


def _rmsnorm_kernel(x_ref, g_ref, o_ref):
    o_ref[...] = _rms_normalize(x_ref[...], g_ref[...]).astype(o_ref.dtype)


def rmsnorm(x, g, out_dtype, tm=512):
    t, d = x.shape
    tm = min(tm, t)
    return pl.pallas_call(
        _rmsnorm_kernel,
        grid=(t // tm,),
        in_specs=[pl.BlockSpec((tm, d), lambda i: (i, 0)),
                  pl.BlockSpec((1, d), lambda i: (0, 0))],
        out_specs=pl.BlockSpec((tm, d), lambda i: (i, 0)),
        out_shape=jax.ShapeDtypeStruct((t, d), out_dtype),
        name="rmsnorm",
        compiler_params=_cparams(("parallel",)),
    )(x, g.reshape(1, d))


def _norm_matmul_kernel(x_ref, g_ref, w_ref, o_ref, xn_ref):
    @pl.when(pl.program_id(1) == 0)
    def _():
        xn_ref[...] = _rms_normalize(x_ref[...], g_ref[...]).astype(BF16)

    o_ref[...] = _dot(xn_ref[...], w_ref[...])


def norm_matmul(x, g, w, tm, tn, layer=None):
    m, k = x.shape
    n = w.shape[-1]
    tm = min(tm, m)
    if layer is None:
        w_spec = pl.BlockSpec((k, tn), lambda i, j: (0, j))
    else:
        w_spec = pl.BlockSpec((None, k, tn), lambda i, j: (layer, 0, j))
    return pl.pallas_call(
        _norm_matmul_kernel,
        grid=(m // tm, n // tn),
        in_specs=[pl.BlockSpec((tm, k), lambda i, j: (i, 0)),
                  pl.BlockSpec((1, k), lambda i, j: (0, 0)),
                  w_spec],
        out_specs=pl.BlockSpec((tm, tn), lambda i, j: (i, j)),
        out_shape=jax.ShapeDtypeStruct((m, n), F32),
        scratch_shapes=[pltpu.VMEM((tm, k), BF16)],
        name="inproj",
        compiler_params=_cparams(("parallel", "arbitrary")),
    )(x, g.reshape(1, k), w)


def _outproj_kernel(y_ref, ym_ref, w1_ref, w2_ref, x_ref, o_ref):
    acc = _dot(y_ref[...], w1_ref[...]) + _dot(ym_ref[...], w2_ref[...])
    o_ref[...] = x_ref[...] + acc


def outproj(y, ym, w_all, layer, x, tm=512):
    m, k1 = y.shape
    k2 = ym.shape[1]
    n = w_all.shape[-1]
    tm = min(tm, m)
    return pl.pallas_call(
        _outproj_kernel,
        grid=(m // tm,),
        in_specs=[pl.BlockSpec((tm, k1), lambda i: (i, 0)),
                  pl.BlockSpec((tm, k2), lambda i: (i, 0)),
                  pl.BlockSpec((None, k1, n), lambda i: (layer, 0, 0)),
                  pl.BlockSpec((None, k2, n), lambda i: (layer, k1 // k2, 0)),
                  pl.BlockSpec((tm, n), lambda i: (i, 0))],
        out_specs=pl.BlockSpec((tm, n), lambda i: (i, 0)),
        out_shape=jax.ShapeDtypeStruct((m, n), F32),
        name="outproj",
        compiler_params=_cparams(("parallel",)),
    )(y, ym, w_all, w_all, x)


def _ffn_kernel(x_ref, xh_ref, g_ref, wg_ref, wv_ref, cg_ref, cv_ref, wd_ref, o_ref, hcat_ref,
                *, tm, seq, halo):
    i = pl.program_id(0)
    j = pl.program_id(1)

    @pl.when(j == 0)
    def _():
        g = g_ref[...]
        hh = _rms_normalize(xh_ref[...], g).astype(BF16)
        hcat_ref[0:halo, :] = jnp.where((i * tm) % seq == 0, jnp.zeros_like(hh), hh)
        x = x_ref[...]
        hcat_ref[halo:, :] = _rms_normalize(x, g).astype(BF16)
        o_ref[...] = x

    hc = hcat_ref[...]

    def branch(w_ref, c_ref):
        u = _dot(hc, w_ref[...])
        c = c_ref[...]
        return (c[0:1] * pltpu.roll(u, 2, 0)[halo:]
                + c[1:2] * pltpu.roll(u, 1, 0)[halo:]
                + c[2:3] * u[halo:])

    act = _silu(branch(wg_ref, cg_ref)) * branch(wv_ref, cv_ref)
    o_ref[...] += _dot(act.astype(BF16), wd_ref[...])


def ffn(x, g, w_up, conv, w_down, layer, seq, tm=512, tf=512):
    t, d = x.shape
    f = w_down.shape[1]
    tm = min(tm, seq)
    tf = min(tf, f)
    nf = f // tf
    hb = 2 * SUBLANES
    kern = functools.partial(_ffn_kernel, tm=tm, seq=seq, halo=hb)
    return pl.pallas_call(
        kern,
        grid=(t // tm, nf),
        in_specs=[
            pl.BlockSpec((tm, d), lambda i, j: (i, 0)),
            pl.BlockSpec((hb, d), lambda i, j: (jnp.maximum(i * (tm // hb) - 1, 0), 0)),
            pl.BlockSpec((1, d), lambda i, j: (0, 0)),
            pl.BlockSpec((None, d, tf), lambda i, j: (layer, 0, j)),
            pl.BlockSpec((None, d, tf), lambda i, j: (layer, 0, nf + j)),
            pl.BlockSpec((FFN_CONV, tf), lambda i, j: (0, j)),
            pl.BlockSpec((FFN_CONV, tf), lambda i, j: (0, nf + j)),
            pl.BlockSpec((None, tf, d), lambda i, j: (layer, j, 0)),
        ],
        out_specs=pl.BlockSpec((tm, d), lambda i, j: (i, 0)),
        out_shape=jax.ShapeDtypeStruct((t, d), F32),
        scratch_shapes=[pltpu.VMEM((hb + tm, d), BF16)],
        name="ffn",
        compiler_params=_cparams(("parallel", "arbitrary")),
    )(x, x, g.reshape(1, d), w_up, w_up, conv, conv, w_down)


def _swa_kernel(sink_ref, q_ref, kp_ref, kc_ref, vp_ref, vc_ref, o_ref, bias_ref, *, blk):
    n = pl.program_id(1)
    dh = SWA_HEAD_DIM
    masked = -1e30

    @pl.when(jnp.logical_and(pl.program_id(0) == 0, n == 0))
    def _():
        t = lax.broadcasted_iota(jnp.int32, (blk, 2 * blk), 0)
        j = lax.broadcasted_iota(jnp.int32, (blk, 2 * blk), 1)
        dist = t - j + blk
        valid = (dist >= 0) & (dist < SWA_WINDOW)
        distf = dist.astype(F32)
        for h in range(SWA_Q_HEADS):
            slope = 2.0 ** (-8.0 * (h + 1.0) / SWA_Q_HEADS)
            bias = jnp.where(valid, -slope * distf, masked)
            bias_ref[1, h] = bias
            bias_ref[0, h] = jnp.where(j >= blk, bias, masked)

    slot = jnp.minimum(n, 1)
    scale = dh ** -0.5
    kv = []
    for kh in range(SWA_KV_HEADS):
        ks = slice(kh * dh, (kh + 1) * dh)
        kv.append((jnp.concatenate([kp_ref[0, :, ks], kc_ref[0, :, ks]], axis=0).astype(BF16),
                   jnp.concatenate([vp_ref[0, :, ks], vc_ref[0, :, ks]], axis=0).astype(BF16)))
    for h0 in range(0, SWA_Q_HEADS, SWA_HEADS_PER_STAGE):
        hs = list(range(h0, h0 + SWA_HEADS_PER_STAGE))
        sink = [sink_ref[h] for h in hs]
        q = [(q_ref[0, :, h * dh:(h + 1) * dh] * scale).astype(BF16) for h in hs]
        s = [_dot_nt(qi, kv[h // SWA_GROUP][0]) + bias_ref[slot, h] for qi, h in zip(q, hs)]
        m = [jnp.maximum(jnp.max(si, axis=-1, keepdims=True), sk) for si, sk in zip(s, sink)]
        p = [jnp.exp(si - mi) for si, mi in zip(s, m)]
        denom = [jnp.sum(pi, axis=-1, keepdims=True) + jnp.exp(sk - mi)
                 for pi, sk, mi in zip(p, sink, m)]
        o = [_dot(pi.astype(BF16), kv[h // SWA_GROUP][1]) / di for pi, di, h in zip(p, denom, hs)]
        for h, oi in zip(hs, o):
            o_ref[0, :, h * dh:(h + 1) * dh] = oi.astype(o_ref.dtype)


def swa_attention(p3, sinks, q_col, k_col, v_col):
    b, s, _ = p3.shape
    blk = SWA_WINDOW
    kvw = SWA_KV_HEADS * SWA_HEAD_DIM
    kern = functools.partial(_swa_kernel, blk=blk)
    prev = lambda n: jnp.maximum(n - 1, 0)
    return pl.pallas_call(
        kern,
        grid=(b, s // blk),
        in_specs=[
            pl.BlockSpec(memory_space=pltpu.SMEM),
            pl.BlockSpec((1, blk, MIX_WIDTH), lambda bi, n: (bi, n, q_col // MIX_WIDTH)),
            pl.BlockSpec((1, blk, kvw), lambda bi, n: (bi, prev(n), k_col // kvw)),
            pl.BlockSpec((1, blk, kvw), lambda bi, n: (bi, n, k_col // kvw)),
            pl.BlockSpec((1, blk, kvw), lambda bi, n: (bi, prev(n), v_col // kvw)),
            pl.BlockSpec((1, blk, kvw), lambda bi, n: (bi, n, v_col // kvw)),
        ],
        out_specs=pl.BlockSpec((1, blk, MIX_WIDTH), lambda bi, n: (bi, n, 0)),
        out_shape=jax.ShapeDtypeStruct((b, s, MIX_WIDTH), BF16),
        scratch_shapes=[pltpu.VMEM((2, SWA_Q_HEADS, blk, 2 * blk), F32)],
        name="swa",
        compiler_params=_cparams(("arbitrary", "arbitrary")),
    )(sinks, p3, p3, p3, p3, p3)


def _memattn_kernel(q_ref, kv_ref, o_ref):
    dh = MEM_HEAD_DIM
    scale = dh ** -0.5
    for h in range(MEM_HEADS):
        q = q_ref[0, :, h * dh:(h + 1) * dh].astype(BF16)
        k = kv_ref[0, :, h * dh:(h + 1) * dh].astype(BF16)
        v = kv_ref[0, :, MEM_WIDTH + h * dh:MEM_WIDTH + (h + 1) * dh].astype(BF16)
        s = _dot_nt(q, k) * scale
        m = jnp.max(s, axis=-1, keepdims=True)
        p = jnp.exp(s - m)
        denom = jnp.sum(p, axis=-1, keepdims=True)
        o = _dot(p.astype(BF16), v) / denom
        o_ref[0, :, h * dh:(h + 1) * dh] = o.astype(o_ref.dtype)


def mem_attention(p3, q_col, mem_kv3, tq=512):
    b, s, _ = p3.shape
    m = mem_kv3.shape[1]
    tq = min(tq, s)
    return pl.pallas_call(
        _memattn_kernel,
        grid=(b, s // tq),
        in_specs=[
            pl.BlockSpec((1, tq, MEM_WIDTH), lambda bi, i: (bi, i, q_col // MEM_WIDTH)),
            pl.BlockSpec((1, m, 2 * MEM_WIDTH), lambda bi, i: (bi, 0, 0)),
        ],
        out_specs=pl.BlockSpec((1, tq, MEM_WIDTH), lambda bi, i: (bi, i, 0)),
        out_shape=jax.ShapeDtypeStruct((b, s, MEM_WIDTH), BF16),
        name="memattn",
        compiler_params=_cparams(("parallel", "parallel")),
    )(p3, mem_kv3)


RW_RKV = 0
RW_QMEM = 3 * MIX_WIDTH
RW_LR = RW_QMEM + MEM_WIDTH
RW_LR_WIDTH = 512
RW_N = RW_LR + RW_LR_WIDTH


def _rwkv_pre_kernel(rkv_ref, lr_ref, rkv_h_ref, lr_h_ref, mu_rkv_ref, mu_lr_ref,
                     w0_ref, wdu_ref, a0_ref, wiu_ref, wgu_ref, kk_ref, ka_ref,
                     r_o, k_o, v_o, a_o, b_o, lw_o, g_o, *, tm, seq):
    i = pl.program_id(0)
    seq_start = (i * tm) % seq == 0

    def mixed(x_ref, h_ref, mu_ref):
        x = x_ref[...]
        prev = _shift_rows(x, h_ref[...], 1, seq_start)
        return x + (prev - x) * mu_ref[...]

    rkv = mixed(rkv_ref, rkv_h_ref, mu_rkv_ref)
    lr = mixed(lr_ref, lr_h_ref, mu_lr_ref)
    w = MIX_WIDTH
    r = rkv[:, :w]
    k = rkv[:, w:2 * w]
    v = rkv[:, 2 * w:]
    wd = jnp.tanh(lr[:, 0:LANES]).astype(BF16)
    ad = lr[:, LANES:2 * LANES].astype(BF16)
    gd = _sigmoid(lr[:, 2 * LANES:]).astype(BF16)
    z = w0_ref[...] + _dot(wd, wdu_ref[...])
    w_log = -_softplus(-z) - 0.5
    lw = -jnp.exp(w_log)
    a_sig = _sigmoid(a0_ref[...] + _dot(ad, wiu_ref[...]))
    g = _dot(gd, wgu_ref[...])
    ones64 = _group_ones(RWKV_HEAD_DIM)
    kk = k * kk_ref[...]
    kk = kk * lax.rsqrt(_group_sum(kk * kk, ones64) + 1e-6)
    k2 = k * (1.0 + (a_sig - 1.0) * ka_ref[...])
    r_o[...] = r
    k_o[...] = k2
    v_o[...] = v
    a_o[...] = kk
    b_o[...] = kk * a_sig
    lw_o[...] = lw
    g_o[...] = g


def rwkv_pre(p, mu_rkv, mu_lr, w0, wdu, a0, wiu, wgu, k_k, k_a, seq, tm=256):
    t = p.shape[0]
    tm = min(tm, seq)
    w = MIX_WIDTH
    hb = SUBLANES
    kern = functools.partial(_rwkv_pre_kernel, tm=tm, seq=seq)
    row = lambda i: (i, 0)
    halo = lambda i: jnp.maximum(i * (tm // hb) - 1, 0)
    const = lambda i: (0, 0)
    vec = pl.BlockSpec((1, w), const)
    out = pl.BlockSpec((tm, w), row)
    return pl.pallas_call(
        kern,
        grid=(t // tm,),
        in_specs=[
            pl.BlockSpec((tm, 3 * w), lambda i: (i, 0)),
            pl.BlockSpec((tm, RW_LR_WIDTH), lambda i: (i, RW_LR // RW_LR_WIDTH)),
            pl.BlockSpec((hb, 3 * w), lambda i: (halo(i), 0)),
            pl.BlockSpec((hb, RW_LR_WIDTH), lambda i: (halo(i), RW_LR // RW_LR_WIDTH)),
            pl.BlockSpec((1, 3 * w), const),
            pl.BlockSpec((1, RW_LR_WIDTH), const),
            vec,
            pl.BlockSpec((LANES, w), const),
            vec,
            pl.BlockSpec((LANES, w), const),
            pl.BlockSpec((2 * LANES, w), const),
            vec, vec,
        ],
        out_specs=[out] * 7,
        out_shape=[jax.ShapeDtypeStruct((t, w), F32)] * 7,
        name="rwkv_pre",
        compiler_params=_cparams(("parallel",)),
    )(p, p, p, p, mu_rkv, mu_lr, w0, wdu, a0, wiu, wgu, k_k, k_a)


def _expand(x, lane_a):
    return jnp.concatenate([jnp.where(lane_a, x, 0.0), jnp.where(lane_a, 0.0, x)], axis=0)


def _dot3_stacked(lhs, rhs):
    m = lhs.shape[0]
    lh, ll = _split2(lhs)
    rh, rl = _split2(rhs)
    top = _dot(jnp.concatenate([lh, ll], axis=0), rh)
    return top[:m] + top[m:] + _dot(lh, rl)


def _unit_lower_inverses_wide(xs, lane_a):
    c = xs[0].shape[0]
    rows = lax.broadcasted_iota(jnp.int32, xs[0].shape, 0)
    cols = lax.broadcasted_iota(jnp.int32, xs[0].shape, 1) % c
    eye = jnp.where(rows == cols, 1.0, 0.0)
    ts = [eye + x for x in xs]
    ps = [_dot3_stacked(x, _expand(x, lane_a)) for x in xs]
    for _ in range(int(math.log2(c)) - 2):
        prods = [_dot3_stacked(jnp.concatenate([t, p], axis=0), _expand(p, lane_a))
                 for t, p in zip(ts, ps)]
        ts = [t + pr[:c] for t, pr in zip(ts, prods)]
        ps = [pr[c:] for pr in prods]
    return [t + _dot3_stacked(t, _expand(p, lane_a)) for t, p in zip(ts, ps)]


def _rwkv_scan_kernel(r_ref, k_ref, v_ref, a_ref, b_ref, lw_ref, g_ref,
                      rk_ref, gng_ref, gnb_ref, o_ref, state_ref, *, chunk, npairs):
    c = chunk
    hd = RWKV_HEAD_DIM

    @pl.when(pl.program_id(2) == 0)
    def _():
        state_ref[...] = jnp.zeros_like(state_ref)

    ri = lax.broadcasted_iota(jnp.int32, (c, c), 0)
    ci = lax.broadcasted_iota(jnp.int32, (c, c), 1)
    tri = (ri >= ci).astype(BF16)
    rows = lax.broadcasted_iota(jnp.int32, (c, LANES), 0)
    lanes = lax.broadcasted_iota(jnp.int32, (c, LANES), 1)
    lane_a = lanes < hd
    strict = (lanes % hd) < rows
    incl = (lanes % hd) <= rows
    bi = lax.broadcasted_iota(jnp.int32, (LANES, LANES), 0) // hd
    bj = lax.broadcasted_iota(jnp.int32, (LANES, LANES), 1) // hd
    blockdiag = bi == bj
    ones64 = blockdiag.astype(BF16)

    pairs = range(npairs)
    sls = [slice(p * LANES, (p + 1) * LANES) for p in pairs]
    cat = lambda x, y: jnp.concatenate([x, y], axis=0)
    xp = lambda x: _expand(x, lane_a).astype(BF16)
    r = [r_ref[0, :, sl] for sl in sls]
    k = [k_ref[0, :, sl] for sl in sls]
    v = [v_ref[0, :, sl] for sl in sls]
    a = [a_ref[0, :, sl] for sl in sls]
    b = [b_ref[0, :, sl] for sl in sls]
    lw = [lw_ref[0, :, sl] for sl in sls]
    cs_all = _dot_const_lhs(tri, lw_ref[0])
    cs = [cs_all[:, sl] for sl in sls]
    c_last = [x[c - 1:c, :] for x in cs]
    e_neg = [jnp.exp(-x) for x in cs]
    e_end = [jnp.exp(cl - x) for cl, x in zip(c_last, cs)]
    ar = [cat(a[p] * jnp.exp(cs[p] - lw[p]), r[p] * jnp.exp(cs[p])).astype(BF16) for p in pairs]
    s_k = [_dot_nt(ar[p], xp(k[p] * e_neg[p])) for p in pairs]
    s_b = [_dot_nt(ar[p], xp(b[p] * e_neg[p])) for p in pairs]
    tinv = _unit_lower_inverses_wide([-jnp.where(strict, s[:c], 0.0) for s in s_b], lane_a)
    lm_k = [cat(jnp.where(strict, s[:c], 0.0), jnp.where(incl, s[c:], 0.0)).astype(BF16)
            for s in s_k]
    m_rb = [jnp.where(incl, s[c:], 0.0).astype(BF16) for s in s_b]
    state = [state_ref[p] for p in pairs]
    ah = [_dot_nt(ar[p], state[p].astype(BF16)) for p in pairs]
    lv = [_dot(lm_k[p], xp(v[p])) for p in pairs]
    u = [_dot3_stacked(tinv[p], _expand(ah[p][:c] + lv[p][:c], lane_a)) for p in pairs]
    y = [ah[p][c:] + lv[p][c:] - _dot(m_rb[p], xp(u[p])) for p in pairs]
    upd = [_dot_tn(cat(v[p], -u[p]).astype(BF16),
                   cat(k[p] * e_end[p], b[p] * e_end[p]).astype(BF16)) for p in pairs]
    for p in pairs:
        state_ref[p] = jnp.where(blockdiag, state[p] * jnp.exp(c_last[p]) + upd[p], 0.0)

    sums = _group_sums(y + [r[p] * k[p] * rk_ref[0:1, sls[p]] for p in pairs], ones64)
    mean = [x * (1.0 / hd) for x in sums[:npairs]]
    bonus = [x * v[p] for p, x in zip(pairs, sums[npairs:])]
    yc = [x - m for x, m in zip(y, mean)]
    var = [x * (1.0 / hd) for x in _group_sums([x * x for x in yc], ones64)]
    for p in pairs:
        sl = sls[p]
        yn = yc[p] * lax.rsqrt(var[p] + RWKV_GN_EPS) * gng_ref[0:1, sl] + gnb_ref[0:1, sl]
        o_ref[0, :, sl] = ((yn + bonus[p]) * g_ref[0, :, sl]).astype(o_ref.dtype)


def rwkv_scan(r, k, v, a, b, lw, g, r_k, gn_g, gn_b, batch, seq, lane_block=1536):
    w = MIX_WIDTH
    c = RWKV_CHUNK
    lane_block = min(lane_block, w)
    npairs = lane_block // LANES
    ngroups = w // lane_block
    shp = (batch, seq, w)
    args = [z.reshape(shp) for z in (r, k, v, a, b, lw, g)]
    tok = pl.BlockSpec((1, c, lane_block), lambda bi, gi, n: (bi, n, gi))
    vec = pl.BlockSpec((1, lane_block), lambda bi, gi, n: (0, gi))
    kern = functools.partial(_rwkv_scan_kernel, chunk=c, npairs=npairs)
    return pl.pallas_call(
        kern,
        grid=(batch, ngroups, seq // c),
        in_specs=[tok] * 7 + [vec] * 3,
        out_specs=tok,
        out_shape=jax.ShapeDtypeStruct(shp, BF16),
        scratch_shapes=[pltpu.VMEM((npairs, LANES, LANES), F32)],
        name="rwkv_scan",
        compiler_params=_cparams(("parallel", "parallel", "arbitrary")),
    )(*args, r_k, gn_g, gn_b)


GD_QKV = 0
GD_QKV_WIDTH = 2 * GDN_QK_WIDTH + MIX_WIDTH
GD_Z = GD_QKV_WIDTH
GD_QMEM = GD_Z + MIX_WIDTH
GD_BA = GD_QMEM + MEM_WIDTH
GD_N = 5376


def _gdn_pre_kernel(x_ref, ba_ref, xh_ref, cw_ref, alog_ref, dtb_ref,
                    q_o, k_o, v_o, gb_o, *, tm, seq):
    i = pl.program_id(0)
    seq_start = (i * tm) % seq == 0
    x = x_ref[...]
    halo = xh_ref[...]
    cw = cw_ref[...]
    y = cw[GDN_CONV - 1:GDN_CONV] * x
    for s in range(1, GDN_CONV):
        y = y + cw[GDN_CONV - 1 - s:GDN_CONV - s] * _shift_rows(x, halo, s, seq_start)
    y = _silu(y)
    ones128 = jnp.ones((LANES, LANES), BF16)
    qw = GDN_QK_WIDTH
    q = y[:, :qw]
    k = y[:, qw:2 * qw]
    qq, kk = _group_sums([q * q, k * k], ones128)
    q = q * lax.rsqrt(qq + 1e-6) * (GDN_HEAD_DIM ** -0.5)
    k = k * lax.rsqrt(kk + 1e-6)
    q_o[...] = q
    k_o[...] = k
    v_o[...] = y[:, 2 * qw:]
    ba = ba_ref[...]
    lanes = lax.broadcasted_iota(jnp.int32, ba.shape, 1)
    beta = _sigmoid(ba)
    g = -jnp.exp(alog_ref[...]) * _softplus(ba + dtb_ref[...])
    gb_o[...] = jnp.where(lanes < GDN_V_HEADS, beta, g)


def gdn_pre(p, conv_w, alog_row, dtb_row, seq, tm=256):
    t = p.shape[0]
    tm = min(tm, seq)
    hb = SUBLANES
    kern = functools.partial(_gdn_pre_kernel, tm=tm, seq=seq)
    row = lambda i: (i, 0)
    const = lambda i: (0, 0)
    halo = lambda i: jnp.maximum(i * (tm // hb) - 1, 0)
    return pl.pallas_call(
        kern,
        grid=(t // tm,),
        in_specs=[
            pl.BlockSpec((tm, GD_QKV_WIDTH), row),
            pl.BlockSpec((tm, LANES), lambda i: (i, GD_BA // LANES)),
            pl.BlockSpec((hb, GD_QKV_WIDTH), lambda i: (halo(i), 0)),
            pl.BlockSpec((GDN_CONV, GD_QKV_WIDTH), const),
            pl.BlockSpec((1, LANES), const),
            pl.BlockSpec((1, LANES), const),
        ],
        out_specs=[pl.BlockSpec((tm, GDN_QK_WIDTH), row),
                   pl.BlockSpec((tm, GDN_QK_WIDTH), row),
                   pl.BlockSpec((tm, MIX_WIDTH), row),
                   pl.BlockSpec((tm, LANES), row)],
        out_shape=[jax.ShapeDtypeStruct((t, GDN_QK_WIDTH), F32),
                   jax.ShapeDtypeStruct((t, GDN_QK_WIDTH), F32),
                   jax.ShapeDtypeStruct((t, MIX_WIDTH), F32),
                   jax.ShapeDtypeStruct((t, LANES), F32)],
        name="gdn_pre",
        compiler_params=_cparams(("parallel",)),
    )(p, p, p, conv_w, alog_row, dtb_row)


def _gdn_scan_kernel(q_ref, k_ref, v_ref, gb_ref, z_ref, ng_ref, o_ref, state_ref, *, chunk):
    c = chunk
    dh = GDN_HEAD_DIM
    nh = GDN_V_HEADS
    npairs = nh // 2

    @pl.when(pl.program_id(1) == 0)
    def _():
        state_ref[...] = jnp.zeros_like(state_ref)

    ri = lax.broadcasted_iota(jnp.int32, (c, c), 0)
    ci = lax.broadcasted_iota(jnp.int32, (c, c), 1)
    tri = (ri >= ci).astype(BF16)
    ones_cc = jnp.ones((c, c), BF16)
    rows = lax.broadcasted_iota(jnp.int32, (c, LANES), 0)
    lanes = lax.broadcasted_iota(jnp.int32, (c, LANES), 1)
    lane_a = lanes < c
    col = lanes % c
    causal = col <= rows
    strict = col < rows
    diag = col == rows
    zeros = jnp.zeros((c, dh), F32)
    cat = lambda x, y: jnp.concatenate([x, y], axis=0)
    wide = lambda xs, i: jnp.where(lane_a, xs[2 * i], xs[2 * i + 1])

    gb = gb_ref[0]
    gcum = _dot_const_lhs(tri, gb)
    bcol = lambda x, j: jnp.broadcast_to(x[:, j:j + 1], (c, dh))
    beta = [bcol(gb, h) for h in range(nh)]
    gc = [bcol(gcum, nh + h) for h in range(nh)]
    e_gc = [jnp.exp(x) for x in gc]
    g_last = [x[c - 1:c, :] for x in gc]
    q = [q_ref[0, :, i * dh:(i + 1) * dh] for i in range(npairs)]
    k = [k_ref[0, :, i * dh:(i + 1) * dh] for i in range(npairs)]
    v = [v_ref[0, :, h * dh:(h + 1) * dh] for h in range(nh)]

    gcs_w = [wide(gc, i) for i in range(npairs)]
    gcr_all = _dot_const_lhs(ones_cc, jnp.concatenate([jnp.where(diag, x, 0.0) for x in gcs_w], axis=1))
    gcr_w = [gcr_all[:, i * LANES:(i + 1) * LANES] for i in range(npairs)]
    decay_w = [jnp.where(causal, jnp.exp(jnp.minimum(s - r, 0.0)), 0.0)
               for s, r in zip(gcs_w, gcr_w)]
    sc = [_dot_nt(cat(k[i], q[i]).astype(BF16), cat(k[i], k[i]).astype(BF16))
          for i in range(npairs)]
    lmat_w = [jnp.where(strict, sc[i][:c] * wide(beta, i) * decay_w[i], 0.0) for i in range(npairs)]
    tinv = _unit_lower_inverses_wide([-x for x in lmat_w], lane_a)
    a_qk_w = [(sc[i][c:] * decay_w[i]).astype(BF16) for i in range(npairs)]

    def pair_rhs(i):
        ha, hb = 2 * i, 2 * i + 1
        top = jnp.concatenate([v[ha] * beta[ha], zeros, k[i] * (beta[ha] * e_gc[ha]), zeros], axis=1)
        bot = jnp.concatenate([zeros, v[hb] * beta[hb], zeros, k[i] * (beta[hb] * e_gc[hb])], axis=1)
        return cat(top, bot)

    uw = [_dot3_stacked(tinv[i], pair_rhs(i)) for i in range(npairs)]
    u = [uw[h // 2][:, (h % 2) * dh:(h % 2 + 1) * dh] for h in range(nh)]
    w = [uw[h // 2][:, (2 + h % 2) * dh:(3 + h % 2) * dh] for h in range(nh)]
    state = [state_ref[h] for h in range(nh)]
    ws = [_dot(cat(w[h], q[h // 2] * e_gc[h]).astype(BF16), state[h].astype(BF16))
          for h in range(nh)]
    v_new = [u[h] - ws[h][:c] for h in range(nh)]
    vn_bf = [x.astype(BF16) for x in v_new]
    zb = jnp.zeros((c, dh), BF16)
    intra = [_dot(a_qk_w[i],
                  cat(jnp.concatenate([vn_bf[2 * i], zb], axis=1),
                      jnp.concatenate([zb, vn_bf[2 * i + 1]], axis=1)))
             for i in range(npairs)]
    out = [ws[h][c:] + intra[h // 2][:, (h % 2) * dh:(h % 2 + 1) * dh] for h in range(nh)]
    upd = [_dot_tn((k[h // 2] * jnp.exp(g_last[h] - gc[h])).astype(BF16), vn_bf[h])
           for h in range(nh)]
    for h in range(nh):
        state_ref[h] = state[h] * jnp.exp(g_last[h]) + upd[h]
    for h in range(nh):
        ms = jnp.mean(out[h] * out[h], axis=-1, keepdims=True)
        o = out[h] * lax.rsqrt(ms + NORM_EPS) * ng_ref[...]
        o = o * _silu(z_ref[0, :, h * dh:(h + 1) * dh])
        o_ref[0, :, h * dh:(h + 1) * dh] = o.astype(o_ref.dtype)


def gdn_scan(q, k, v, gb, p3, norm_g, batch, seq):
    c = GDN_CHUNK
    qw = GDN_QK_WIDTH
    w = MIX_WIDTH
    kern = functools.partial(_gdn_scan_kernel, chunk=c)
    tokq = pl.BlockSpec((1, c, qw), lambda bi, n: (bi, n, 0))
    tokv = pl.BlockSpec((1, c, w), lambda bi, n: (bi, n, 0))
    return pl.pallas_call(
        kern,
        grid=(batch, seq // c),
        in_specs=[tokq, tokq, tokv,
                  pl.BlockSpec((1, c, LANES), lambda bi, n: (bi, n, 0)),
                  pl.BlockSpec((1, c, w), lambda bi, n: (bi, n, GD_Z // w)),
                  pl.BlockSpec((1, GDN_HEAD_DIM), lambda bi, n: (0, 0))],
        out_specs=tokv,
        out_shape=jax.ShapeDtypeStruct((batch, seq, w), BF16),
        scratch_shapes=[pltpu.VMEM((GDN_V_HEADS, GDN_HEAD_DIM, GDN_HEAD_DIM), F32)],
        name="gdn_scan",
        compiler_params=_cparams(("parallel", "arbitrary")),
    )(q.reshape(batch, seq, qw), k.reshape(batch, seq, qw), v.reshape(batch, seq, w),
      gb.reshape(batch, seq, LANES), p3, norm_g.reshape(1, GDN_HEAD_DIM))


def _pad_cols(w, n):
    return jnp.pad(w, ((0, 0), (0, n - w.shape[1])))


def _pad_rows(w, n):
    return jnp.pad(w, ((0, n - w.shape[0]), (0, 0)))


def _mixer_a(x, g, w_in_all, layer, sinks, batch, seq):
    p = norm_matmul(x, g, w_in_all, tm=1024, tn=512, layer=layer)
    p3 = p.reshape(batch, seq, -1)
    y = swa_attention(p3, sinks, q_col=0, k_col=MIX_WIDTH,
                      v_col=MIX_WIDTH + SWA_KV_HEADS * SWA_HEAD_DIM)
    return y, p3, 2048


def _mixer_b(x, g, w_in, mu, w0, w_decay_up, a0, w_iclr_up, w_gate_up, k_k, k_a, r_k, gn_g, gn_b,
             batch, seq):
    w = MIX_WIDTH
    dr = w_decay_up.shape[0]
    ir = w_iclr_up.shape[0]
    gr = w_gate_up.shape[0]
    o_wd, o_ad, o_gd, o_qm = 3 * w, 3 * w + dr, 3 * w + dr + ir, 3 * w + dr + ir + gr
    w_in_r = jnp.concatenate([
        w_in[:, :3 * w], w_in[:, o_qm:],
        _pad_cols(w_in[:, o_wd:o_ad], LANES), _pad_cols(w_in[:, o_ad:o_gd], LANES),
        w_in[:, o_gd:o_qm]], axis=1).astype(BF16)
    mu_rkv = mu[:3 * w].reshape(1, -1)
    mu_lr = jnp.concatenate([jnp.pad(mu[o_wd:o_ad], (0, LANES - dr)),
                             jnp.pad(mu[o_ad:o_gd], (0, LANES - ir)),
                             mu[o_gd:o_qm]]).reshape(1, -1)
    p = norm_matmul(x, g, w_in_r, tm=1024, tn=512)
    row = lambda z: z.reshape(1, -1)
    r, k, v, a, b, lw, gate = rwkv_pre(
        p, mu_rkv, mu_lr, row(w0), _pad_rows(w_decay_up, LANES).astype(BF16), row(a0),
        _pad_rows(w_iclr_up, LANES).astype(BF16), w_gate_up.astype(BF16), row(k_k), row(k_a), seq)
    y = rwkv_scan(r, k, v, a, b, lw, gate, row(r_k), row(gn_g), row(gn_b), batch, seq)
    return y, p.reshape(batch, seq, -1), RW_QMEM


def _mixer_c(x, g, w_in, conv_w, a_log, dt_bias, norm_g, batch, seq):
    w = MIX_WIDTH
    nh = GDN_V_HEADS
    o_z = GD_QKV_WIDTH
    o_bt = o_z + w
    o_at = o_bt + nh
    o_qm = o_at + nh
    ba = w_in[:, o_bt:o_qm]
    w_in_r = jnp.concatenate([
        w_in[:, :o_z], w_in[:, o_z:o_bt], w_in[:, o_qm:], _pad_cols(ba, LANES)], axis=1)
    w_in_r = _pad_cols(w_in_r, GD_N).astype(BF16)
    p = norm_matmul(x, g, w_in_r, tm=1024, tn=768)
    pad = LANES - 2 * nh
    alog_row = jnp.concatenate([jnp.zeros((nh,), F32), a_log, jnp.zeros((pad,), F32)]).reshape(1, -1)
    dtb_row = jnp.concatenate([jnp.zeros((nh,), F32), dt_bias, jnp.zeros((pad,), F32)]).reshape(1, -1)
    q, k, v, gb = gdn_pre(p, conv_w, alog_row, dtb_row, seq)
    p3 = p.reshape(batch, seq, -1)
    y = gdn_scan(q, k, v, gb, p3, norm_g, batch, seq)
    return y, p3, GD_QMEM


def kernel(x, mem, attn_norm, mem_norm, w_mem_kv, w_out, ffn_norm, w_ffn_up, ffn_conv, w_ffn_down,
           final_norm, a_w_in, a_sinks, b_w_in, b_mu, b_w0, b_w_decay_up, b_a0, b_w_iclr_up,
           b_w_gate_up, b_k_k, b_k_a, b_r_k, b_gn_g, b_gn_b, c_w_in, c_conv, c_a_log, c_dt_bias,
           c_norm_g):
    batch, seq, d = x.shape
    mlen = mem.shape[1]
    depth = attn_norm.shape[0]
    x = x.reshape(batch * seq, d)
    mem2 = mem.reshape(batch * mlen, d)
    w_mem_kv_b = w_mem_kv.astype(BF16)
    w_out_b = w_out.astype(BF16)
    w_up_b = w_ffn_up.astype(BF16)
    w_down_b = w_ffn_down.astype(BF16)
    a_w_in_b = a_w_in.astype(BF16)
    for i in range(depth):
        kind, j = i % 3, i // 3
        g = attn_norm[i]
        mem_kv = norm_matmul(mem2, mem_norm[i], w_mem_kv_b, tm=1024, tn=512, layer=i)
        mem_kv3 = mem_kv.reshape(batch, mlen, -1)
        if kind == 0:
            y, p3, q_col = _mixer_a(x, g, a_w_in_b, j, a_sinks[j], batch, seq)
        elif kind == 1:
            y, p3, q_col = _mixer_b(x, g, b_w_in[j], b_mu[j], b_w0[j], b_w_decay_up[j], b_a0[j],
                                    b_w_iclr_up[j], b_w_gate_up[j], b_k_k[j], b_k_a[j],
                                    b_r_k[j].reshape(-1), b_gn_g[j], b_gn_b[j], batch, seq)
        else:
            y, p3, q_col = _mixer_c(x, g, c_w_in[j], c_conv[j], c_a_log[j], c_dt_bias[j],
                                    c_norm_g[j], batch, seq)
        y_mem = mem_attention(p3, q_col, mem_kv3)
        x = outproj(y.reshape(batch * seq, -1), y_mem.reshape(batch * seq, -1), w_out_b, i, x)
        x = ffn(x, ffn_norm[i], w_up_b, ffn_conv[i], w_down_b, i, seq)
    out = rmsnorm(x, final_norm, F32)
    return out.reshape(batch, seq, d)
```

```python
import functools
import math

import jax
import jax.numpy as jnp
from jax import lax
from jax.experimental import pallas as pl
from jax.experimental.pallas import tpu as pltpu

F32 = jnp.float32
BF16 = jnp.bfloat16

D_MODEL = 2048
DEPTH = 4
MIX_WIDTH = 1536
MEM_HEADS = 4
MEM_HEAD_DIM = 128
MEM_WIDTH = 512
NORM_EPS = 1e-6
SWA_HEAD_DIM = 64
SWA_Q_HEADS = 24
SWA_KV_HEADS = 4
SWA_GROUP = 6
SWA_WINDOW = 128
SWA_HEADS_PER_STAGE = 12
RWKV_HEAD_DIM = 64
RWKV_GN_EPS = 64e-5
RWKV_CHUNK = 64
INVERSE_BASE_BLOCK = 8
GDN_HEAD_DIM = 128
GDN_V_HEADS = 12
GDN_QK_WIDTH = 768
GDN_CONV = 4
GDN_CHUNK = 64
D_FF = 5632
FFN_CONV = 3

LANES = 128
SUBLANES = 8
VMEM_LIMIT_BYTES = 56 * 1024 * 1024


def _cparams(semantics):
    return pltpu.CompilerParams(dimension_semantics=semantics,
                                vmem_limit_bytes=VMEM_LIMIT_BYTES)


def _dot(a, b):
    return jnp.dot(a, b, preferred_element_type=F32)


def _dot_nt(a, b):
    return lax.dot_general(a, b, (((1,), (1,)), ((), ())), preferred_element_type=F32)


def _dot_tn(a, b):
    return lax.dot_general(a, b, (((0,), (0,)), ((), ())), preferred_element_type=F32)


def _split2(x):
    hi = x.astype(BF16)
    lo = (x - hi.astype(F32)).astype(BF16)
    return hi, lo


def _split3(x):
    hi = x.astype(BF16)
    r1 = x - hi.astype(F32)
    mid = r1.astype(BF16)
    lo = (r1 - mid.astype(F32)).astype(BF16)
    return hi, mid, lo


def _dot_const_lhs(c_bf16, x):
    hi, mid, lo = _split3(x)
    return _dot(c_bf16, hi) + _dot(c_bf16, mid) + _dot(c_bf16, lo)


def _sigmoid(x):
    return 1.0 / (1.0 + jnp.exp(-x))


def _silu(x):
    return x * _sigmoid(x)


def _softplus(x):
    return jnp.maximum(x, 0.0) + jnp.log(1.0 + jnp.exp(-jnp.abs(x)))


def _group_sums(xs, group_ones):
    pieces = []
    for x in xs:
        for c in range(x.shape[1] // LANES):
            pieces.extend(_split3(x[:, c * LANES:(c + 1) * LANES]))
    prod = _dot(jnp.concatenate(pieces, axis=0), group_ones)
    outs, row = [], 0
    for x in xs:
        r = x.shape[0]
        cols = []
        for _ in range(x.shape[1] // LANES):
            cols.append(prod[row:row + r] + prod[row + r:row + 2 * r] + prod[row + 2 * r:row + 3 * r])
            row += 3 * r
        outs.append(cols[0] if len(cols) == 1 else jnp.concatenate(cols, axis=1))
    return outs


def _group_sum(x, group_ones):
    return _group_sums([x], group_ones)[0]


def _group_ones(group):
    i = lax.broadcasted_iota(jnp.int32, (LANES, LANES), 0) // group
    j = lax.broadcasted_iota(jnp.int32, (LANES, LANES), 1) // group
    return (i == j).astype(BF16)


def _shift_rows(x, halo, s, seq_start):
    rolled = pltpu.roll(x, s, 0)
    hr = pltpu.roll(halo, s, 0)
    hr = jnp.where(seq_start, 0.0, hr)
    rows = lax.broadcasted_iota(jnp.int32, hr.shape, 0)
    head = jnp.where(rows < s, hr, rolled[:SUBLANES])
    return jnp.concatenate([head, rolled[SUBLANES:]], axis=0)


def _rms_normalize(x, g):
    ms = jnp.mean(x * x, axis=-1, keepdims=True)
    return x * lax.rsqrt(ms + NORM_EPS) * g


def _rmsnorm_kernel(x_ref, g_ref, o_ref):
    o_ref[...] = _rms_normalize(x_ref[...], g_ref[...]).astype(o_ref.dtype)


def rmsnorm(x, g, out_dtype, tm=512):
    t, d = x.shape
    tm = min(tm, t)
    return pl.pallas_call(
        _rmsnorm_kernel,
        grid=(t // tm,),
        in_specs=[pl.BlockSpec((tm, d), lambda i: (i, 0)),
                  pl.BlockSpec((1, d), lambda i: (0, 0))],
        out_specs=pl.BlockSpec((tm, d), lambda i: (i, 0)),
        out_shape=jax.ShapeDtypeStruct((t, d), out_dtype),
        name="rmsnorm",
        compiler_params=_cparams(("parallel",)),
    )(x, g.reshape(1, d))


def _norm_matmul_kernel(x_ref, g_ref, w_ref, o_ref, xn_ref):
    @pl.when(pl.program_id(1) == 0)
    def _():
        xn_ref[...] = _rms_normalize(x_ref[...], g_ref[...]).astype(BF16)

    o_ref[...] = _dot(xn_ref[...], w_ref[...])


def norm_matmul(x, g, w, tm, tn, layer=None):
    m, k = x.shape
    n = w.shape[-1]
    tm = min(tm, m)
    if layer is None:
        w_spec = pl.BlockSpec((k, tn), lambda i, j: (0, j))
    else:
        w_spec = pl.BlockSpec((None, k, tn), lambda i, j: (layer, 0, j))
    return pl.pallas_call(
        _norm_matmul_kernel,
        grid=(m // tm, n // tn),
        in_specs=[pl.BlockSpec((tm, k), lambda i, j: (i, 0)),
                  pl.BlockSpec((1, k), lambda i, j: (0, 0)),
                  w_spec],
        out_specs=pl.BlockSpec((tm, tn), lambda i, j: (i, j)),
        out_shape=jax.ShapeDtypeStruct((m, n), F32),
        scratch_shapes=[pltpu.VMEM((tm, k), BF16)],
        name="inproj",
        compiler_params=_cparams(("parallel", "arbitrary")),
    )(x, g.reshape(1, k), w)


def _outproj_kernel(y_ref, ym_ref, w1_ref, w2_ref, x_ref, o_ref):
    acc = _dot(y_ref[...], w1_ref[...]) + _dot(ym_ref[...], w2_ref[...])
    o_ref[...] = x_ref[...] + acc


def outproj(y, ym, w_all, layer, x, tm=512):
    m, k1 = y.shape
    k2 = ym.shape[1]
    n = w_all.shape[-1]
    tm = min(tm, m)
    return pl.pallas_call(
        _outproj_kernel,
        grid=(m // tm,),
        in_specs=[pl.BlockSpec((tm, k1), lambda i: (i, 0)),
                  pl.BlockSpec((tm, k2), lambda i: (i, 0)),
                  pl.BlockSpec((None, k1, n), lambda i: (layer, 0, 0)),
                  pl.BlockSpec((None, k2, n), lambda i: (layer, k1 // k2, 0)),
                  pl.BlockSpec((tm, n), lambda i: (i, 0))],
        out_specs=pl.BlockSpec((tm, n), lambda i: (i, 0)),
        out_shape=jax.ShapeDtypeStruct((m, n), F32),
        name="outproj",
        compiler_params=_cparams(("parallel",)),
    )(y, ym, w_all, w_all, x)


def _ffn_kernel(x_ref, xh_ref, g_ref, wg_ref, wv_ref, cg_ref, cv_ref, wd_ref, o_ref, hcat_ref,
                *, tm, seq, halo):
    i = pl.program_id(0)
    j = pl.program_id(1)

    @pl.when(j == 0)
    def _():
        g = g_ref[...]
        hh = _rms_normalize(xh_ref[...], g).astype(BF16)
        hcat_ref[0:halo, :] = jnp.where((i * tm) % seq == 0, jnp.zeros_like(hh), hh)
        x = x_ref[...]
        hcat_ref[halo:, :] = _rms_normalize(x, g).astype(BF16)
        o_ref[...] = x

    hc = hcat_ref[...]

    def branch(w_ref, c_ref):
        u = _dot(hc, w_ref[...])
        c = c_ref[...]
        return (c[0:1] * pltpu.roll(u, 2, 0)[halo:]
                + c[1:2] * pltpu.roll(u, 1, 0)[halo:]
                + c[2:3] * u[halo:])

    act = _silu(branch(wg_ref, cg_ref)) * branch(wv_ref, cv_ref)
    o_ref[...] += _dot(act.astype(BF16), wd_ref[...])


def ffn(x, g, w_up, conv, w_down, layer, seq, tm=512, tf=512):
    t, d = x.shape
    f = w_down.shape[1]
    tm = min(tm, seq)
    tf = min(tf, f)
    nf = f // tf
    hb = 2 * SUBLANES
    kern = functools.partial(_ffn_kernel, tm=tm, seq=seq, halo=hb)
    return pl.pallas_call(
        kern,
        grid=(t // tm, nf),
        in_specs=[
            pl.BlockSpec((tm, d), lambda i, j: (i, 0)),
            pl.BlockSpec((hb, d), lambda i, j: (jnp.maximum(i * (tm // hb) - 1, 0), 0)),
            pl.BlockSpec((1, d), lambda i, j: (0, 0)),
            pl.BlockSpec((None, d, tf), lambda i, j: (layer, 0, j)),
            pl.BlockSpec((None, d, tf), lambda i, j: (layer, 0, nf + j)),
            pl.BlockSpec((FFN_CONV, tf), lambda i, j: (0, j)),
            pl.BlockSpec((FFN_CONV, tf), lambda i, j: (0, nf + j)),
            pl.BlockSpec((None, tf, d), lambda i, j: (layer, j, 0)),
        ],
        out_specs=pl.BlockSpec((tm, d), lambda i, j: (i, 0)),
        out_shape=jax.ShapeDtypeStruct((t, d), F32),
        scratch_shapes=[pltpu.VMEM((hb + tm, d), BF16)],
        name="ffn",
        compiler_params=_cparams(("parallel", "arbitrary")),
    )(x, x, g.reshape(1, d), w_up, w_up, conv, conv, w_down)


def _swa_kernel(sink_ref, q_ref, kp_ref, kc_ref, vp_ref, vc_ref, o_ref, bias_ref, *, blk):
    n = pl.program_id(1)
    dh = SWA_HEAD_DIM
    masked = -1e30

    @pl.when(jnp.logical_and(pl.program_id(0) == 0, n == 0))
    def _():
        t = lax.broadcasted_iota(jnp.int32, (blk, 2 * blk), 0)
        j = lax.broadcasted_iota(jnp.int32, (blk, 2 * blk), 1)
        dist = t - j + blk
        valid = (dist >= 0) & (dist < SWA_WINDOW)
        distf = dist.astype(F32)
        for h in range(SWA_Q_HEADS):
            slope = 2.0 ** (-8.0 * (h + 1.0) / SWA_Q_HEADS)
            bias = jnp.where(valid, -slope * distf, masked)
            bias_ref[1, h] = bias
            bias_ref[0, h] = jnp.where(j >= blk, bias, masked)

    slot = jnp.minimum(n, 1)
    scale = dh ** -0.5
    kv = []
    for kh in range(SWA_KV_HEADS):
        ks = slice(kh * dh, (kh + 1) * dh)
        kv.append((jnp.concatenate([kp_ref[0, :, ks], kc_ref[0, :, ks]], axis=0).astype(BF16),
                   jnp.concatenate([vp_ref[0, :, ks], vc_ref[0, :, ks]], axis=0).astype(BF16)))
    for h0 in range(0, SWA_Q_HEADS, SWA_HEADS_PER_STAGE):
        hs = list(range(h0, h0 + SWA_HEADS_PER_STAGE))
        sink = [sink_ref[h] for h in hs]
        q = [(q_ref[0, :, h * dh:(h + 1) * dh] * scale).astype(BF16) for h in hs]
        s = [_dot_nt(qi, kv[h // SWA_GROUP][0]) + bias_ref[slot, h] for qi, h in zip(q, hs)]
        m = [jnp.maximum(jnp.max(si, axis=-1, keepdims=True), sk) for si, sk in zip(s, sink)]
        p = [jnp.exp(si - mi) for si, mi in zip(s, m)]
        denom = [jnp.sum(pi, axis=-1, keepdims=True) + jnp.exp(sk - mi)
                 for pi, sk, mi in zip(p, sink, m)]
        o = [_dot(pi.astype(BF16), kv[h // SWA_GROUP][1]) / di for pi, di, h in zip(p, denom, hs)]
        for h, oi in zip(hs, o):
            o_ref[0, :, h * dh:(h + 1) * dh] = oi.astype(o_ref.dtype)


def swa_attention(p3, sinks, q_col, k_col, v_col):
    b, s, _ = p3.shape
    blk = SWA_WINDOW
    kvw = SWA_KV_HEADS * SWA_HEAD_DIM
    kern = functools.partial(_swa_kernel, blk=blk)
    prev = lambda n: jnp.maximum(n - 1, 0)
    return pl.pallas_call(
        kern,
        grid=(b, s // blk),
        in_specs=[
            pl.BlockSpec(memory_space=pltpu.SMEM),
            pl.BlockSpec((1, blk, MIX_WIDTH), lambda bi, n: (bi, n, q_col // MIX_WIDTH)),
            pl.BlockSpec((1, blk, kvw), lambda bi, n: (bi, prev(n), k_col // kvw)),
            pl.BlockSpec((1, blk, kvw), lambda bi, n: (bi, n, k_col // kvw)),
            pl.BlockSpec((1, blk, kvw), lambda bi, n: (bi, prev(n), v_col // kvw)),
            pl.BlockSpec((1, blk, kvw), lambda bi, n: (bi, n, v_col // kvw)),
        ],
        out_specs=pl.BlockSpec((1, blk, MIX_WIDTH), lambda bi, n: (bi, n, 0)),
        out_shape=jax.ShapeDtypeStruct((b, s, MIX_WIDTH), BF16),
        scratch_shapes=[pltpu.VMEM((2, SWA_Q_HEADS, blk, 2 * blk), F32)],
        name="swa",
        compiler_params=_cparams(("arbitrary", "arbitrary")),
    )(sinks, p3, p3, p3, p3, p3)


def _memattn_kernel(q_ref, kv_ref, o_ref):
    dh = MEM_HEAD_DIM
    scale = dh ** -0.5
    for h in range(MEM_HEADS):
        q = q_ref[0, :, h * dh:(h + 1) * dh].astype(BF16)
        k = kv_ref[0, :, h * dh:(h + 1) * dh].astype(BF16)
        v = kv_ref[0, :, MEM_WIDTH + h * dh:MEM_WIDTH + (h + 1) * dh].astype(BF16)
        s = _dot_nt(q, k) * scale
        m = jnp.max(s, axis=-1, keepdims=True)
        p = jnp.exp(s - m)
        denom = jnp.sum(p, axis=-1, keepdims=True)
        o = _dot(p.astype(BF16), v) / denom
        o_ref[0, :, h * dh:(h + 1) * dh] = o.astype(o_ref.dtype)


def mem_attention(p3, q_col, mem_kv3, tq=512):
    b, s, _ = p3.shape
    m = mem_kv3.shape[1]
    tq = min(tq, s)
    return pl.pallas_call(
        _memattn_kernel,
        grid=(b, s // tq),
        in_specs=[
            pl.BlockSpec((1, tq, MEM_WIDTH), lambda bi, i: (bi, i, q_col // MEM_WIDTH)),
            pl.BlockSpec((1, m, 2 * MEM_WIDTH), lambda bi, i: (bi, 0, 0)),
        ],
        out_specs=pl.BlockSpec((1, tq, MEM_WIDTH), lambda bi, i: (bi, i, 0)),
        out_shape=jax.ShapeDtypeStruct((b, s, MEM_WIDTH), BF16),
        name="memattn",
        compiler_params=_cparams(("parallel", "parallel")),
    )(p3, mem_kv3)


RW_RKV = 0
RW_QMEM = 3 * MIX_WIDTH
RW_LR = RW_QMEM + MEM_WIDTH
RW_LR_WIDTH = 512
RW_N = RW_LR + RW_LR_WIDTH


def _rwkv_pre_kernel(rkv_ref, lr_ref, rkv_h_ref, lr_h_ref, mu_rkv_ref, mu_lr_ref,
                     w0_ref, wdu_ref, a0_ref, wiu_ref, wgu_ref, kk_ref, ka_ref,
                     r_o, k_o, v_o, a_o, b_o, lw_o, g_o, *, tm, seq):
    i = pl.program_id(0)
    seq_start = (i * tm) % seq == 0

    def mixed(x_ref, h_ref, mu_ref):
        x = x_ref[...]
        prev = _shift_rows(x, h_ref[...], 1, seq_start)
        return x + (prev - x) * mu_ref[...]

    rkv = mixed(rkv_ref, rkv_h_ref, mu_rkv_ref)
    lr = mixed(lr_ref, lr_h_ref, mu_lr_ref)
    w = MIX_WIDTH
    r = rkv[:, :w]
    k = rkv[:, w:2 * w]
    v = rkv[:, 2 * w:]
    wd = jnp.tanh(lr[:, 0:LANES]).astype(BF16)
    ad = lr[:, LANES:2 * LANES].astype(BF16)
    gd = _sigmoid(lr[:, 2 * LANES:]).astype(BF16)
    z = w0_ref[...] + _dot(wd, wdu_ref[...])
    w_log = -_softplus(-z) - 0.5
    lw = -jnp.exp(w_log)
    a_sig = _sigmoid(a0_ref[...] + _dot(ad, wiu_ref[...]))
    g = _dot(gd, wgu_ref[...])
    ones64 = _group_ones(RWKV_HEAD_DIM)
    kk = k * kk_ref[...]
    kk = kk * lax.rsqrt(_group_sum(kk * kk, ones64) + 1e-6)
    k2 = k * (1.0 + (a_sig - 1.0) * ka_ref[...])
    r_o[...] = r
    k_o[...] = k2
    v_o[...] = v
    a_o[...] = kk
    b_o[...] = kk * a_sig
    lw_o[...] = lw
    g_o[...] = g


def rwkv_pre(p, mu_rkv, mu_lr, w0, wdu, a0, wiu, wgu, k_k, k_a, seq, tm=256):
    t = p.shape[0]
    tm = min(tm, seq)
    w = MIX_WIDTH
    hb = SUBLANES
    kern = functools.partial(_rwkv_pre_kernel, tm=tm, seq=seq)
    row = lambda i: (i, 0)
    halo = lambda i: jnp.maximum(i * (tm // hb) - 1, 0)
    const = lambda i: (0, 0)
    vec = pl.BlockSpec((1, w), const)
    out = pl.BlockSpec((tm, w), row)
    return pl.pallas_call(
        kern,
        grid=(t // tm,),
        in_specs=[
            pl.BlockSpec((tm, 3 * w), lambda i: (i, 0)),
            pl.BlockSpec((tm, RW_LR_WIDTH), lambda i: (i, RW_LR // RW_LR_WIDTH)),
            pl.BlockSpec((hb, 3 * w), lambda i: (halo(i), 0)),
            pl.BlockSpec((hb, RW_LR_WIDTH), lambda i: (halo(i), RW_LR // RW_LR_WIDTH)),
            pl.BlockSpec((1, 3 * w), const),
            pl.BlockSpec((1, RW_LR_WIDTH), const),
            vec,
            pl.BlockSpec((LANES, w), const),
            vec,
            pl.BlockSpec((LANES, w), const),
            pl.BlockSpec((2 * LANES, w), const),
            vec, vec,
        ],
        out_specs=[out] * 7,
        out_shape=[jax.ShapeDtypeStruct((t, w), F32)] * 7,
        name="rwkv_pre",
        compiler_params=_cparams(("parallel",)),
    )(p, p, p, p, mu_rkv, mu_lr, w0, wdu, a0, wiu, wgu, k_k, k_a)


def _expand(x, lane_a):
    return jnp.concatenate([jnp.where(lane_a, x, 0.0), jnp.where(lane_a, 0.0, x)], axis=0)


def _dot3_stacked(lhs, rhs):
    m = lhs.shape[0]
    lh, ll = _split2(lhs)
    rh, rl = _split2(rhs)
    top = _dot(jnp.concatenate([lh, ll], axis=0), rh)
    return top[:m] + top[m:] + _dot(lh, rl)


def _unit_lower_inverses_wide(ls, lane_a):
    c = ls[0].shape[0]
    rows = lax.broadcasted_iota(jnp.int32, ls[0].shape, 0)
    cols = lax.broadcasted_iota(jnp.int32, ls[0].shape, 1) % c
    eye = jnp.where(rows == cols, 1.0, 0.0)
    ex = lambda x: _expand(x, lane_a)
    cat = lambda x, y: jnp.concatenate([x, y], axis=0)
    b = INVERSE_BASE_BLOCK
    same = (rows // b) == (cols // b)
    xs = [jnp.where(same, -l, 0.0) for l in ls]
    ts = [eye + x for x in xs]
    ps = [_dot3_stacked(x, ex(x)) for x in xs]
    prods = [_dot3_stacked(cat(t, p), ex(p)) for t, p in zip(ts, ps)]
    ts = [t + pr[:c] for t, pr in zip(ts, prods)]
    ds = [t + _dot3_stacked(t, ex(pr[c:])) for t, pr in zip(ts, prods)]
    while b < c:
        sub = ((rows // (2 * b)) == (cols // (2 * b))) & ((rows // b) != (cols // b))
        ys = [_dot3_stacked(jnp.where(sub, l, 0.0), ex(d)) for l, d in zip(ls, ds)]
        ds = [d - _dot3_stacked(d, ex(y)) for d, y in zip(ds, ys)]
        b *= 2
    return ds


def _rwkv_scan_kernel(r_ref, k_ref, v_ref, a_ref, b_ref, lw_ref, g_ref,
                      rk_ref, gng_ref, gnb_ref, o_ref, state_ref, *, chunk, npairs):
    c = chunk
    hd = RWKV_HEAD_DIM

    @pl.when(pl.program_id(2) == 0)
    def _():
        state_ref[...] = jnp.zeros_like(state_ref)

    ri = lax.broadcasted_iota(jnp.int32, (c, c), 0)
    ci = lax.broadcasted_iota(jnp.int32, (c, c), 1)
    tri = (ri >= ci).astype(BF16)
    rows = lax.broadcasted_iota(jnp.int32, (c, LANES), 0)
    lanes = lax.broadcasted_iota(jnp.int32, (c, LANES), 1)
    lane_a = lanes < hd
    strict = (lanes % hd) < rows
    incl = (lanes % hd) <= rows
    bi = lax.broadcasted_iota(jnp.int32, (LANES, LANES), 0) // hd
    bj = lax.broadcasted_iota(jnp.int32, (LANES, LANES), 1) // hd
    blockdiag = bi == bj
    ones64 = blockdiag.astype(BF16)

    pairs = range(npairs)
    sls = [slice(p * LANES, (p + 1) * LANES) for p in pairs]
    cat = lambda x, y: jnp.concatenate([x, y], axis=0)
    xp = lambda x: _expand(x, lane_a).astype(BF16)
    r = [r_ref[0, :, sl] for sl in sls]
    k = [k_ref[0, :, sl] for sl in sls]
    v = [v_ref[0, :, sl] for sl in sls]
    a = [a_ref[0, :, sl] for sl in sls]
    b = [b_ref[0, :, sl] for sl in sls]
    lw = [lw_ref[0, :, sl] for sl in sls]
    cs_all = _dot_const_lhs(tri, lw_ref[0])
    cs = [cs_all[:, sl] for sl in sls]
    c_last = [x[c - 1:c, :] for x in cs]
    e_neg = [jnp.exp(-x) for x in cs]
    e_end = [jnp.exp(cl - x) for cl, x in zip(c_last, cs)]
    ar = [cat(a[p] * jnp.exp(cs[p] - lw[p]), r[p] * jnp.exp(cs[p])).astype(BF16) for p in pairs]
    s_k = [_dot_nt(ar[p], xp(k[p] * e_neg[p])) for p in pairs]
    s_b = [_dot_nt(ar[p], xp(b[p] * e_neg[p])) for p in pairs]
    tinv = _unit_lower_inverses_wide([jnp.where(strict, s[:c], 0.0) for s in s_b], lane_a)
    lm_k = [cat(jnp.where(strict, s[:c], 0.0), jnp.where(incl, s[c:], 0.0)).astype(BF16)
            for s in s_k]
    m_rb = [jnp.where(incl, s[c:], 0.0).astype(BF16) for s in s_b]
    state = [state_ref[p] for p in pairs]
    ah = [_dot_nt(ar[p], state[p].astype(BF16)) for p in pairs]
    lv = [_dot(lm_k[p], xp(v[p])) for p in pairs]
    u = [_dot3_stacked(tinv[p], _expand(ah[p][:c] + lv[p][:c], lane_a)) for p in pairs]
    y = [ah[p][c:] + lv[p][c:] - _dot(m_rb[p], xp(u[p])) for p in pairs]
    upd = [_dot_tn(cat(v[p], -u[p]).astype(BF16),
                   cat(k[p] * e_end[p], b[p] * e_end[p]).astype(BF16)) for p in pairs]
    for p in pairs:
        state_ref[p] = jnp.where(blockdiag, state[p] * jnp.exp(c_last[p]) + upd[p], 0.0)

    sums = _group_sums(y + [r[p] * k[p] * rk_ref[0:1, sls[p]] for p in pairs], ones64)
    mean = [x * (1.0 / hd) for x in sums[:npairs]]
    bonus = [x * v[p] for p, x in zip(pairs, sums[npairs:])]
    yc = [x - m for x, m in zip(y, mean)]
    var = [x * (1.0 / hd) for x in _group_sums([x * x for x in yc], ones64)]
    for p in pairs:
        sl = sls[p]
        yn = yc[p] * lax.rsqrt(var[p] + RWKV_GN_EPS) * gng_ref[0:1, sl] + gnb_ref[0:1, sl]
        o_ref[0, :, sl] = ((yn + bonus[p]) * g_ref[0, :, sl]).astype(o_ref.dtype)


def rwkv_scan(r, k, v, a, b, lw, g, r_k, gn_g, gn_b, batch, seq, lane_block=1536):
    w = MIX_WIDTH
    c = RWKV_CHUNK
    lane_block = min(lane_block, w)
    npairs = lane_block // LANES
    ngroups = w // lane_block
    shp = (batch, seq, w)
    args = [z.reshape(shp) for z in (r, k, v, a, b, lw, g)]
    tok = pl.BlockSpec((1, c, lane_block), lambda bi, gi, n: (bi, n, gi))
    vec = pl.BlockSpec((1, lane_block), lambda bi, gi, n: (0, gi))
    kern = functools.partial(_rwkv_scan_kernel, chunk=c, npairs=npairs)
    return pl.pallas_call(
        kern,
        grid=(batch, ngroups, seq // c),
        in_specs=[tok] * 7 + [vec] * 3,
        out_specs=tok,
        out_shape=jax.ShapeDtypeStruct(shp, BF16),
        scratch_shapes=[pltpu.VMEM((npairs, LANES, LANES), F32)],
        name="rwkv_scan",
        compiler_params=_cparams(("parallel", "parallel", "arbitrary")),
    )(*args, r_k, gn_g, gn_b)


GD_QKV = 0
GD_QKV_WIDTH = 2 * GDN_QK_WIDTH + MIX_WIDTH
GD_Z = GD_QKV_WIDTH
GD_QMEM = GD_Z + MIX_WIDTH
GD_BA = GD_QMEM + MEM_WIDTH
GD_N = 5376


def _gdn_pre_kernel(x_ref, ba_ref, xh_ref, cw_ref, alog_ref, dtb_ref,
                    q_o, k_o, v_o, gb_o, *, tm, seq):
    i = pl.program_id(0)
    seq_start = (i * tm) % seq == 0
    x = x_ref[...]
    halo = xh_ref[...]
    cw = cw_ref[...]
    y = cw[GDN_CONV - 1:GDN_CONV] * x
    for s in range(1, GDN_CONV):
        y = y + cw[GDN_CONV - 1 - s:GDN_CONV - s] * _shift_rows(x, halo, s, seq_start)
    y = _silu(y)
    ones128 = jnp.ones((LANES, LANES), BF16)
    qw = GDN_QK_WIDTH
    q = y[:, :qw]
    k = y[:, qw:2 * qw]
    qq, kk = _group_sums([q * q, k * k], ones128)
    q = q * lax.rsqrt(qq + 1e-6) * (GDN_HEAD_DIM ** -0.5)
    k = k * lax.rsqrt(kk + 1e-6)
    q_o[...] = q
    k_o[...] = k
    v_o[...] = y[:, 2 * qw:]
    ba = ba_ref[...]
    lanes = lax.broadcasted_iota(jnp.int32, ba.shape, 1)
    beta = _sigmoid(ba)
    g = -jnp.exp(alog_ref[...]) * _softplus(ba + dtb_ref[...])
    gb_o[...] = jnp.where(lanes < GDN_V_HEADS, beta, g)


def gdn_pre(p, conv_w, alog_row, dtb_row, seq, tm=256):
    t = p.shape[0]
    tm = min(tm, seq)
    hb = SUBLANES
    kern = functools.partial(_gdn_pre_kernel, tm=tm, seq=seq)
    row = lambda i: (i, 0)
    const = lambda i: (0, 0)
    halo = lambda i: jnp.maximum(i * (tm // hb) - 1, 0)
    return pl.pallas_call(
        kern,
        grid=(t // tm,),
        in_specs=[
            pl.BlockSpec((tm, GD_QKV_WIDTH), row),
            pl.BlockSpec((tm, LANES), lambda i: (i, GD_BA // LANES)),
            pl.BlockSpec((hb, GD_QKV_WIDTH), lambda i: (halo(i), 0)),
            pl.BlockSpec((GDN_CONV, GD_QKV_WIDTH), const),
            pl.BlockSpec((1, LANES), const),
            pl.BlockSpec((1, LANES), const),
        ],
        out_specs=[pl.BlockSpec((tm, GDN_QK_WIDTH), row),
                   pl.BlockSpec((tm, GDN_QK_WIDTH), row),
                   pl.BlockSpec((tm, MIX_WIDTH), row),
                   pl.BlockSpec((tm, LANES), row)],
        out_shape=[jax.ShapeDtypeStruct((t, GDN_QK_WIDTH), F32),
                   jax.ShapeDtypeStruct((t, GDN_QK_WIDTH), F32),
                   jax.ShapeDtypeStruct((t, MIX_WIDTH), F32),
                   jax.ShapeDtypeStruct((t, LANES), F32)],
        name="gdn_pre",
        compiler_params=_cparams(("parallel",)),
    )(p, p, p, conv_w, alog_row, dtb_row)


def _gdn_scan_kernel(q_ref, k_ref, v_ref, gb_ref, z_ref, ng_ref, o_ref, state_ref, *, chunk):
    c = chunk
    dh = GDN_HEAD_DIM
    nh = GDN_V_HEADS
    npairs = nh // 2

    @pl.when(pl.program_id(1) == 0)
    def _():
        state_ref[...] = jnp.zeros_like(state_ref)

    ri = lax.broadcasted_iota(jnp.int32, (c, c), 0)
    ci = lax.broadcasted_iota(jnp.int32, (c, c), 1)
    tri = (ri >= ci).astype(BF16)
    ones_cc = jnp.ones((c, c), BF16)
    rows = lax.broadcasted_iota(jnp.int32, (c, LANES), 0)
    lanes = lax.broadcasted_iota(jnp.int32, (c, LANES), 1)
    lane_a = lanes < c
    col = lanes % c
    causal = col <= rows
    strict = col < rows
    diag = col == rows
    zeros = jnp.zeros((c, dh), F32)
    cat = lambda x, y: jnp.concatenate([x, y], axis=0)
    wide = lambda xs, i: jnp.where(lane_a, xs[2 * i], xs[2 * i + 1])

    gb = gb_ref[0]
    gcum = _dot_const_lhs(tri, gb)
    bcol = lambda x, j: jnp.broadcast_to(x[:, j:j + 1], (c, dh))
    beta = [bcol(gb, h) for h in range(nh)]
    gc = [bcol(gcum, nh + h) for h in range(nh)]
    e_gc = [jnp.exp(x) for x in gc]
    g_last = [x[c - 1:c, :] for x in gc]
    q = [q_ref[0, :, i * dh:(i + 1) * dh] for i in range(npairs)]
    k = [k_ref[0, :, i * dh:(i + 1) * dh] for i in range(npairs)]
    v = [v_ref[0, :, h * dh:(h + 1) * dh] for h in range(nh)]

    gcs_w = [wide(gc, i) for i in range(npairs)]
    gcr_all = _dot_const_lhs(ones_cc, jnp.concatenate([jnp.where(diag, x, 0.0) for x in gcs_w], axis=1))
    gcr_w = [gcr_all[:, i * LANES:(i + 1) * LANES] for i in range(npairs)]
    decay_w = [jnp.where(causal, jnp.exp(jnp.minimum(s - r, 0.0)), 0.0)
               for s, r in zip(gcs_w, gcr_w)]
    sc = [_dot_nt(cat(k[i], q[i]).astype(BF16), cat(k[i], k[i]).astype(BF16))
          for i in range(npairs)]
    lmat_w = [jnp.where(strict, sc[i][:c] * wide(beta, i) * decay_w[i], 0.0) for i in range(npairs)]
    tinv = _unit_lower_inverses_wide(lmat_w, lane_a)
    a_qk_w = [(sc[i][c:] * decay_w[i]).astype(BF16) for i in range(npairs)]

    def pair_rhs(i):
        ha, hb = 2 * i, 2 * i + 1
        top = jnp.concatenate([v[ha] * beta[ha], zeros, k[i] * (beta[ha] * e_gc[ha]), zeros], axis=1)
        bot = jnp.concatenate([zeros, v[hb] * beta[hb], zeros, k[i] * (beta[hb] * e_gc[hb])], axis=1)
        return cat(top, bot)

    uw = [_dot3_stacked(tinv[i], pair_rhs(i)) for i in range(npairs)]
    u = [uw[h // 2][:, (h % 2) * dh:(h % 2 + 1) * dh] for h in range(nh)]
    w = [uw[h // 2][:, (2 + h % 2) * dh:(3 + h % 2) * dh] for h in range(nh)]
    state = [state_ref[h] for h in range(nh)]
    ws = [_dot(cat(w[h], q[h // 2] * e_gc[h]).astype(BF16), state[h].astype(BF16))
          for h in range(nh)]
    v_new = [u[h] - ws[h][:c] for h in range(nh)]
    vn_bf = [x.astype(BF16) for x in v_new]
    zb = jnp.zeros((c, dh), BF16)
    intra = [_dot(a_qk_w[i],
                  cat(jnp.concatenate([vn_bf[2 * i], zb], axis=1),
                      jnp.concatenate([zb, vn_bf[2 * i + 1]], axis=1)))
             for i in range(npairs)]
    out = [ws[h][c:] + intra[h // 2][:, (h % 2) * dh:(h % 2 + 1) * dh] for h in range(nh)]
    upd = [_dot_tn((k[h // 2] * jnp.exp(g_last[h] - gc[h])).astype(BF16), vn_bf[h])
           for h in range(nh)]
    for h in range(nh):
        state_ref[h] = state[h] * jnp.exp(g_last[h]) + upd[h]
    for h in range(nh):
        ms = jnp.mean(out[h] * out[h], axis=-1, keepdims=True)
        o = out[h] * lax.rsqrt(ms + NORM_EPS) * ng_ref[...]
        o = o * _silu(z_ref[0, :, h * dh:(h + 1) * dh])
        o_ref[0, :, h * dh:(h + 1) * dh] = o.astype(o_ref.dtype)


def gdn_scan(q, k, v, gb, p3, norm_g, batch, seq):
    c = GDN_CHUNK
    qw = GDN_QK_WIDTH
    w = MIX_WIDTH
    kern = functools.partial(_gdn_scan_kernel, chunk=c)
    tokq = pl.BlockSpec((1, c, qw), lambda bi, n: (bi, n, 0))
    tokv = pl.BlockSpec((1, c, w), lambda bi, n: (bi, n, 0))
    return pl.pallas_call(
        kern,
        grid=(batch, seq // c),
        in_specs=[tokq, tokq, tokv,
                  pl.BlockSpec((1, c, LANES), lambda bi, n: (bi, n, 0)),
                  pl.BlockSpec((1, c, w), lambda bi, n: (bi, n, GD_Z // w)),
                  pl.BlockSpec((1, GDN_HEAD_DIM), lambda bi, n: (0, 0))],
        out_specs=tokv,
        out_shape=jax.ShapeDtypeStruct((batch, seq, w), BF16),
        scratch_shapes=[pltpu.VMEM((GDN_V_HEADS, GDN_HEAD_DIM, GDN_HEAD_DIM), F32)],
        name="gdn_scan",
        compiler_params=_cparams(("parallel", "arbitrary")),
    )(q.reshape(batch, seq, qw), k.reshape(batch, seq, qw), v.reshape(batch, seq, w),
      gb.reshape(batch, seq, LANES), p3, norm_g.reshape(1, GDN_HEAD_DIM))


def _pad_cols(w, n):
    return jnp.pad(w, ((0, 0), (0, n - w.shape[1])))


def _pad_rows(w, n):
    return jnp.pad(w, ((0, n - w.shape[0]), (0, 0)))


def _mixer_a(x, g, w_in_all, layer, sinks, batch, seq):
    p = norm_matmul(x, g, w_in_all, tm=1024, tn=512, layer=layer)
    p3 = p.reshape(batch, seq, -1)
    y = swa_attention(p3, sinks, q_col=0, k_col=MIX_WIDTH,
                      v_col=MIX_WIDTH + SWA_KV_HEADS * SWA_HEAD_DIM)
    return y, p3, 2048


def _mixer_b(x, g, w_in, mu, w0, w_decay_up, a0, w_iclr_up, w_gate_up, k_k, k_a, r_k, gn_g, gn_b,
             batch, seq):
    w = MIX_WIDTH
    dr = w_decay_up.shape[0]
    ir = w_iclr_up.shape[0]
    gr = w_gate_up.shape[0]
    o_wd, o_ad, o_gd, o_qm = 3 * w, 3 * w + dr, 3 * w + dr + ir, 3 * w + dr + ir + gr
    w_in_r = jnp.concatenate([
        w_in[:, :3 * w], w_in[:, o_qm:],
        _pad_cols(w_in[:, o_wd:o_ad], LANES), _pad_cols(w_in[:, o_ad:o_gd], LANES),
        w_in[:, o_gd:o_qm]], axis=1).astype(BF16)
    mu_rkv = mu[:3 * w].reshape(1, -1)
    mu_lr = jnp.concatenate([jnp.pad(mu[o_wd:o_ad], (0, LANES - dr)),
                             jnp.pad(mu[o_ad:o_gd], (0, LANES - ir)),
                             mu[o_gd:o_qm]]).reshape(1, -1)
    p = norm_matmul(x, g, w_in_r, tm=1024, tn=512)
    row = lambda z: z.reshape(1, -1)
    r, k, v, a, b, lw, gate = rwkv_pre(
        p, mu_rkv, mu_lr, row(w0), _pad_rows(w_decay_up, LANES).astype(BF16), row(a0),
        _pad_rows(w_iclr_up, LANES).astype(BF16), w_gate_up.astype(BF16), row(k_k), row(k_a), seq)
    y = rwkv_scan(r, k, v, a, b, lw, gate, row(r_k), row(gn_g), row(gn_b), batch, seq)
    return y, p.reshape(batch, seq, -1), RW_QMEM


def _mixer_c(x, g, w_in, conv_w, a_log, dt_bias, norm_g, batch, seq):
    w = MIX_WIDTH
    nh = GDN_V_HEADS
    o_z = GD_QKV_WIDTH
    o_bt = o_z + w
    o_at = o_bt + nh
    o_qm = o_at + nh
    ba = w_in[:, o_bt:o_qm]
    w_in_r = jnp.concatenate([
        w_in[:, :o_z], w_in[:, o_z:o_bt], w_in[:, o_qm:], _pad_cols(ba, LANES)], axis=1)
    w_in_r = _pad_cols(w_in_r, GD_N).astype(BF16)
    p = norm_matmul(x, g, w_in_r, tm=1024, tn=768)
    pad = LANES - 2 * nh
    alog_row = jnp.concatenate([jnp.zeros((nh,), F32), a_log, jnp.zeros((pad,), F32)]).reshape(1, -1)
    dtb_row = jnp.concatenate([jnp.zeros((nh,), F32), dt_bias, jnp.zeros((pad,), F32)]).reshape(1, -1)
    q, k, v, gb = gdn_pre(p, conv_w, alog_row, dtb_row, seq)
    p3 = p.reshape(batch, seq, -1)
    y = gdn_scan(q, k, v, gb, p3, norm_g, batch, seq)
    return y, p3, GD_QMEM


def kernel(x, mem, attn_norm, mem_norm, w_mem_kv, w_out, ffn_norm, w_ffn_up, ffn_conv, w_ffn_down,
           final_norm, a_w_in, a_sinks, b_w_in, b_mu, b_w0, b_w_decay_up, b_a0, b_w_iclr_up,
           b_w_gate_up, b_k_k, b_k_a, b_r_k, b_gn_g, b_gn_b, c_w_in, c_conv, c_a_log, c_dt_bias,
           c_norm_g):
    batch, seq, d = x.shape
    mlen = mem.shape[1]
    depth = attn_norm.shape[0]
    x = x.reshape(batch * seq, d)
    mem2 = mem.reshape(batch * mlen, d)
    w_mem_kv_b = w_mem_kv.astype(BF16)
    w_out_b = w_out.astype(BF16)
    w_up_b = w_ffn_up.astype(BF16)
    w_down_b = w_ffn_down.astype(BF16)
    a_w_in_b = a_w_in.astype(BF16)
    for i in range(depth):
        kind, j = i % 3, i // 3
        g = attn_norm[i]
        mem_kv = norm_matmul(mem2, mem_norm[i], w_mem_kv_b, tm=1024, tn=512, layer=i)
        mem_kv3 = mem_kv.reshape(batch, mlen, -1)
        if kind == 0:
            y, p3, q_col = _mixer_a(x, g, a_w_in_b, j, a_sinks[j], batch, seq)
        elif kind == 1:
            y, p3, q_col = _mixer_b(x, g, b_w_in[j], b_mu[j], b_w0[j], b_w_decay_up[j], b_a0[j],
                                    b_w_iclr_up[j], b_w_gate_up[j], b_k_k[j], b_k_a[j],
                                    b_r_k[j].reshape(-1), b_gn_g[j], b_gn_b[j], batch, seq)
        else:
            y, p3, q_col = _mixer_c(x, g, c_w_in[j], c_conv[j], c_a_log[j], c_dt_bias[j],
                                    c_norm_g[j], batch, seq)
        y_mem = mem_attention(p3, q_col, mem_kv3)
        x = outproj(y.reshape(batch * seq, -1), y_mem.reshape(batch * seq, -1), w_out_b, i, x)
        x = ffn(x, ffn_norm[i], w_up_b, ffn_conv[i], w_down_b, i, seq)
    out = rmsnorm(x, final_norm, F32)
    return out.reshape(batch, seq, d)
```

```python
import functools
import math

import jax
import jax.numpy as jnp
from jax import lax
from jax.experimental import pallas as pl
from jax.experimental.pallas import tpu as pltpu

F32 = jnp.float32
BF16 = jnp.bfloat16

D_MODEL = 2048
DEPTH = 4
MIX_WIDTH = 1536
MEM_HEADS = 4
MEM_HEAD_DIM = 128
MEM_WIDTH = 512
NORM_EPS = 1e-6
SWA_HEAD_DIM = 64
SWA_Q_HEADS = 24
SWA_KV_HEADS = 4
SWA_GROUP = 6
SWA_WINDOW = 128
SWA_HEADS_PER_STAGE = 12
RWKV_HEAD_DIM = 64
RWKV_GN_EPS = 64e-5
RWKV_CHUNK = 64
INVERSE_BASE_BLOCK = 8
GDN_HEAD_DIM = 128
GDN_V_HEADS = 12
GDN_QK_WIDTH = 768
GDN_CONV = 4
GDN_CHUNK = 64
D_FF = 5632
FFN_CONV = 3

LANES = 128
SUBLANES = 8
VMEM_LIMIT_BYTES = 56 * 1024 * 1024


def _cparams(semantics):
    return pltpu.CompilerParams(dimension_semantics=semantics,
                                vmem_limit_bytes=VMEM_LIMIT_BYTES)


def _dot(a, b):
    return jnp.dot(a, b, preferred_element_type=F32)


def _dot_nt(a, b):
    return lax.dot_general(a, b, (((1,), (1,)), ((), ())), preferred_element_type=F32)


def _dot_tn(a, b):
    return lax.dot_general(a, b, (((0,), (0,)), ((), ())), preferred_element_type=F32)


def _split2(x):
    hi = x.astype(BF16)
    lo = (x - hi.astype(F32)).astype(BF16)
    return hi, lo


def _split3(x):
    hi = x.astype(BF16)
    r1 = x - hi.astype(F32)
    mid = r1.astype(BF16)
    lo = (r1 - mid.astype(F32)).astype(BF16)
    return hi, mid, lo


def _dot_const_lhs(c_bf16, x):
    hi, mid, lo = _split3(x)
    return _dot(c_bf16, hi) + _dot(c_bf16, mid) + _dot(c_bf16, lo)


def _sigmoid(x):
    return 1.0 / (1.0 + jnp.exp(-x))


def _silu(x):
    return x * _sigmoid(x)


def _softplus(x):
    return jnp.maximum(x, 0.0) + jnp.log(1.0 + jnp.exp(-jnp.abs(x)))


def _group_sums(xs, group_ones):
    pieces = []
    for x in xs:
        for c in range(x.shape[1] // LANES):
            pieces.extend(_split3(x[:, c * LANES:(c + 1) * LANES]))
    prod = _dot(jnp.concatenate(pieces, axis=0), group_ones)
    outs, row = [], 0
    for x in xs:
        r = x.shape[0]
        cols = []
        for _ in range(x.shape[1] // LANES):
            cols.append(prod[row:row + r] + prod[row + r:row + 2 * r] + prod[row + 2 * r:row + 3 * r])
            row += 3 * r
        outs.append(cols[0] if len(cols) == 1 else jnp.concatenate(cols, axis=1))
    return outs


def _group_sum(x, group_ones):
    return _group_sums([x], group_ones)[0]


def _group_ones(group):
    i = lax.broadcasted_iota(jnp.int32, (LANES, LANES), 0) // group
    j = lax.broadcasted_iota(jnp.int32, (LANES, LANES), 1) // group
    return (i == j).astype(BF16)


def _shift_rows(x, halo, s, seq_start):
    rolled = pltpu.roll(x, s, 0)
    hr = pltpu.roll(halo, s, 0)
    hr = jnp.where(seq_start, 0.0, hr)
    rows = lax.broadcasted_iota(jnp.int32, hr.shape, 0)
    head = jnp.where(rows < s, hr, rolled[:SUBLANES])
    return jnp.concatenate([head, rolled[SUBLANES:]], axis=0)


def _rms_normalize(x, g):
    ms = jnp.mean(x * x, axis=-1, keepdims=True)
    return x * lax.rsqrt(ms + NORM_EPS) * g


def _norm_matmul_kernel(x_ref, g_ref, w_ref, o_ref, xn_ref):
    @pl.when(pl.program_id(1) == 0)
    def _():
        xn_ref[...] = _rms_normalize(x_ref[...], g_ref[...]).astype(BF16)

    o_ref[...] = _dot(xn_ref[...], w_ref[...]).astype(o_ref.dtype)


def norm_matmul(x, g, w, tm, tn, layer=None, out_dtype=F32):
    m, k = x.shape
    n = w.shape[-1]
    tm = min(tm, m)
    if layer is None:
        w_spec = pl.BlockSpec((k, tn), lambda i, j: (0, j))
    else:
        w_spec = pl.BlockSpec((None, k, tn), lambda i, j: (layer, 0, j))
    return pl.pallas_call(
        _norm_matmul_kernel,
        grid=(m // tm, n // tn),
        in_specs=[pl.BlockSpec((tm, k), lambda i, j: (i, 0)),
                  pl.BlockSpec((1, k), lambda i, j: (0, 0)),
                  w_spec],
        out_specs=pl.BlockSpec((tm, tn), lambda i, j: (i, j)),
        out_shape=jax.ShapeDtypeStruct((m, n), out_dtype),
        scratch_shapes=[pltpu.VMEM((tm, k), BF16)],
        name="inproj",
        compiler_params=_cparams(("parallel", "arbitrary")),
    )(x, g.reshape(1, k), w)


def _outproj_kernel(y_ref, ym_ref, w1_ref, w2_ref, x_ref, o_ref):
    acc = _dot(y_ref[...], w1_ref[...]) + _dot(ym_ref[...], w2_ref[...])
    o_ref[...] = x_ref[...] + acc


def outproj(y, ym, w_all, layer, x, tm=512):
    m, k1 = y.shape
    k2 = ym.shape[1]
    n = w_all.shape[-1]
    tm = min(tm, m)
    return pl.pallas_call(
        _outproj_kernel,
        grid=(m // tm,),
        in_specs=[pl.BlockSpec((tm, k1), lambda i: (i, 0)),
                  pl.BlockSpec((tm, k2), lambda i: (i, 0)),
                  pl.BlockSpec((None, k1, n), lambda i: (layer, 0, 0)),
                  pl.BlockSpec((None, k2, n), lambda i: (layer, k1 // k2, 0)),
                  pl.BlockSpec((tm, n), lambda i: (i, 0))],
        out_specs=pl.BlockSpec((tm, n), lambda i: (i, 0)),
        out_shape=jax.ShapeDtypeStruct((m, n), F32),
        name="outproj",
        compiler_params=_cparams(("parallel",)),
    )(y, ym, w_all, w_all, x)


def _ffn_kernel(x_ref, xh_ref, g_ref, gf_ref, wg_ref, wv_ref, cg_ref, cv_ref, wd_ref, o_ref, hcat_ref,
                *, tm, seq, halo, final_norm):
    i = pl.program_id(0)
    j = pl.program_id(1)

    @pl.when(j == 0)
    def _():
        g = g_ref[...]
        hh = _rms_normalize(xh_ref[...], g).astype(BF16)
        hcat_ref[0:halo, :] = jnp.where((i * tm) % seq == 0, jnp.zeros_like(hh), hh)
        x = x_ref[...]
        hcat_ref[halo:, :] = _rms_normalize(x, g).astype(BF16)
        o_ref[...] = x

    hc = hcat_ref[...]

    def branch(w_ref, c_ref):
        u = _dot(hc, w_ref[...])
        c = c_ref[...]
        return (c[0:1] * pltpu.roll(u, 2, 0)[halo:]
                + c[1:2] * pltpu.roll(u, 1, 0)[halo:]
                + c[2:3] * u[halo:])

    act = _silu(branch(wg_ref, cg_ref)) * branch(wv_ref, cv_ref)
    o_ref[...] += _dot(act.astype(BF16), wd_ref[...])

    if final_norm:
        @pl.when(j == pl.num_programs(1) - 1)
        def _():
            o_ref[...] = _rms_normalize(o_ref[...], gf_ref[...])


def ffn(x, g, w_up, conv, w_down, layer, seq, g_final, final_norm, tm=512, tf=512):
    t, d = x.shape
    f = w_down.shape[1]
    tm = min(tm, seq)
    tf = min(tf, f)
    nf = f // tf
    hb = 2 * SUBLANES
    kern = functools.partial(_ffn_kernel, tm=tm, seq=seq, halo=hb, final_norm=final_norm)
    return pl.pallas_call(
        kern,
        grid=(t // tm, nf),
        in_specs=[
            pl.BlockSpec((tm, d), lambda i, j: (i, 0)),
            pl.BlockSpec((hb, d), lambda i, j: (jnp.maximum(i * (tm // hb) - 1, 0), 0)),
            pl.BlockSpec((1, d), lambda i, j: (0, 0)),
            pl.BlockSpec((1, d), lambda i, j: (0, 0)),
            pl.BlockSpec((None, d, tf), lambda i, j: (layer, 0, j)),
            pl.BlockSpec((None, d, tf), lambda i, j: (layer, 0, nf + j)),
            pl.BlockSpec((FFN_CONV, tf), lambda i, j: (0, j)),
            pl.BlockSpec((FFN_CONV, tf), lambda i, j: (0, nf + j)),
            pl.BlockSpec((None, tf, d), lambda i, j: (layer, j, 0)),
        ],
        out_specs=pl.BlockSpec((tm, d), lambda i, j: (i, 0)),
        out_shape=jax.ShapeDtypeStruct((t, d), F32),
        scratch_shapes=[pltpu.VMEM((hb + tm, d), BF16)],
        name="ffn",
        compiler_params=_cparams(("parallel", "arbitrary")),
    )(x, x, g.reshape(1, d), g_final.reshape(1, d), w_up, w_up, conv, conv, w_down)


def _swa_kernel(sink_ref, q_ref, kp_ref, kc_ref, vp_ref, vc_ref, o_ref, bias_ref, *, blk):
    n = pl.program_id(1)
    dh = SWA_HEAD_DIM
    masked = -1e30

    @pl.when(jnp.logical_and(pl.program_id(0) == 0, n == 0))
    def _():
        t = lax.broadcasted_iota(jnp.int32, (blk, 2 * blk), 0)
        j = lax.broadcasted_iota(jnp.int32, (blk, 2 * blk), 1)
        dist = t - j + blk
        valid = (dist >= 0) & (dist < SWA_WINDOW)
        distf = dist.astype(F32)
        for h in range(SWA_Q_HEADS):
            slope = 2.0 ** (-8.0 * (h + 1.0) / SWA_Q_HEADS)
            bias = jnp.where(valid, -slope * distf, masked)
            bias_ref[1, h] = bias
            bias_ref[0, h] = jnp.where(j >= blk, bias, masked)

    slot = jnp.minimum(n, 1)
    scale = dh ** -0.5
    kv = []
    for kh in range(SWA_KV_HEADS):
        ks = slice(kh * dh, (kh + 1) * dh)
        kv.append((jnp.concatenate([kp_ref[0, :, ks], kc_ref[0, :, ks]], axis=0).astype(BF16),
                   jnp.concatenate([vp_ref[0, :, ks], vc_ref[0, :, ks]], axis=0).astype(BF16)))
    for h0 in range(0, SWA_Q_HEADS, SWA_HEADS_PER_STAGE):
        hs = list(range(h0, h0 + SWA_HEADS_PER_STAGE))
        sink = [sink_ref[h] for h in hs]
        q = [(q_ref[0, :, h * dh:(h + 1) * dh] * scale).astype(BF16) for h in hs]
        s = [_dot_nt(qi, kv[h // SWA_GROUP][0]) + bias_ref[slot, h] for qi, h in zip(q, hs)]
        m = [jnp.maximum(jnp.max(si, axis=-1, keepdims=True), sk) for si, sk in zip(s, sink)]
        p = [jnp.exp(si - mi) for si, mi in zip(s, m)]
        denom = [jnp.sum(pi, axis=-1, keepdims=True) + jnp.exp(sk - mi)
                 for pi, sk, mi in zip(p, sink, m)]
        o = [_dot(pi.astype(BF16), kv[h // SWA_GROUP][1]) / di for pi, di, h in zip(p, denom, hs)]
        for h, oi in zip(hs, o):
            o_ref[0, :, h * dh:(h + 1) * dh] = oi.astype(o_ref.dtype)


def swa_attention(p3, sinks, q_col, k_col, v_col):
    b, s, _ = p3.shape
    blk = SWA_WINDOW
    kvw = SWA_KV_HEADS * SWA_HEAD_DIM
    kern = functools.partial(_swa_kernel, blk=blk)
    prev = lambda n: jnp.maximum(n - 1, 0)
    return pl.pallas_call(
        kern,
        grid=(b, s // blk),
        in_specs=[
            pl.BlockSpec(memory_space=pltpu.SMEM),
            pl.BlockSpec((1, blk, MIX_WIDTH), lambda bi, n: (bi, n, q_col // MIX_WIDTH)),
            pl.BlockSpec((1, blk, kvw), lambda bi, n: (bi, prev(n), k_col // kvw)),
            pl.BlockSpec((1, blk, kvw), lambda bi, n: (bi, n, k_col // kvw)),
            pl.BlockSpec((1, blk, kvw), lambda bi, n: (bi, prev(n), v_col // kvw)),
            pl.BlockSpec((1, blk, kvw), lambda bi, n: (bi, n, v_col // kvw)),
        ],
        out_specs=pl.BlockSpec((1, blk, MIX_WIDTH), lambda bi, n: (bi, n, 0)),
        out_shape=jax.ShapeDtypeStruct((b, s, MIX_WIDTH), BF16),
        scratch_shapes=[pltpu.VMEM((2, SWA_Q_HEADS, blk, 2 * blk), F32)],
        name="swa",
        compiler_params=_cparams(("arbitrary", "arbitrary")),
    )(sinks, p3, p3, p3, p3, p3)


def _memattn_kernel(q_ref, kv_ref, o_ref):
    dh = MEM_HEAD_DIM
    scale = dh ** -0.5
    for h in range(MEM_HEADS):
        q = q_ref[0, :, h * dh:(h + 1) * dh].astype(BF16)
        k = kv_ref[0, :, h * dh:(h + 1) * dh].astype(BF16)
        v = kv_ref[0, :, MEM_WIDTH + h * dh:MEM_WIDTH + (h + 1) * dh].astype(BF16)
        s = _dot_nt(q, k) * scale
        m = jnp.max(s, axis=-1, keepdims=True)
        p = jnp.exp(s - m)
        denom = jnp.sum(p, axis=-1, keepdims=True)
        o = _dot(p.astype(BF16), v) / denom
        o_ref[0, :, h * dh:(h + 1) * dh] = o.astype(o_ref.dtype)


def mem_attention(p3, q_col, mem_kv3, tq=512):
    b, s, _ = p3.shape
    m = mem_kv3.shape[1]
    tq = min(tq, s)
    return pl.pallas_call(
        _memattn_kernel,
        grid=(b, s // tq),
        in_specs=[
            pl.BlockSpec((1, tq, MEM_WIDTH), lambda bi, i: (bi, i, q_col // MEM_WIDTH)),
            pl.BlockSpec((1, m, 2 * MEM_WIDTH), lambda bi, i: (bi, 0, 0)),
        ],
        out_specs=pl.BlockSpec((1, tq, MEM_WIDTH), lambda bi, i: (bi, i, 0)),
        out_shape=jax.ShapeDtypeStruct((b, s, MEM_WIDTH), BF16),
        name="memattn",
        compiler_params=_cparams(("parallel", "parallel")),
    )(p3, mem_kv3)


RW_RKV = 0
RW_QMEM = 3 * MIX_WIDTH
RW_LR = RW_QMEM + MEM_WIDTH
RW_LR_WIDTH = 512
RW_N = RW_LR + RW_LR_WIDTH


def _rwkv_pre_kernel(rkv_ref, lr_ref, rkv_h_ref, lr_h_ref, mu_rkv_ref, mu_lr_ref,
                     w0_ref, wdu_ref, a0_ref, wiu_ref, wgu_ref, kk_ref, ka_ref,
                     r_o, k_o, v_o, a_o, b_o, lw_o, g_o, *, tm, seq):
    i = pl.program_id(0)
    seq_start = (i * tm) % seq == 0

    def mixed(x_ref, h_ref, mu_ref):
        x = x_ref[...]
        prev = _shift_rows(x, h_ref[...], 1, seq_start)
        return x + (prev - x) * mu_ref[...]

    rkv = mixed(rkv_ref, rkv_h_ref, mu_rkv_ref)
    lr = mixed(lr_ref, lr_h_ref, mu_lr_ref)
    w = MIX_WIDTH
    r = rkv[:, :w]
    k = rkv[:, w:2 * w]
    v = rkv[:, 2 * w:]
    wd = jnp.tanh(lr[:, 0:LANES]).astype(BF16)
    ad = lr[:, LANES:2 * LANES].astype(BF16)
    gd = _sigmoid(lr[:, 2 * LANES:]).astype(BF16)
    z = w0_ref[...] + _dot(wd, wdu_ref[...])
    w_log = -_softplus(-z) - 0.5
    lw = -jnp.exp(w_log)
    a_sig = _sigmoid(a0_ref[...] + _dot(ad, wiu_ref[...]))
    g = _dot(gd, wgu_ref[...])
    ones64 = _group_ones(RWKV_HEAD_DIM)
    kk = k * kk_ref[...]
    kk = kk * lax.rsqrt(_group_sum(kk * kk, ones64) + 1e-6)
    k2 = k * (1.0 + (a_sig - 1.0) * ka_ref[...])
    r_o[...] = r
    k_o[...] = k2
    v_o[...] = v
    a_o[...] = kk
    b_o[...] = kk * a_sig
    lw_o[...] = lw
    g_o[...] = g


def rwkv_pre(p, mu_rkv, mu_lr, w0, wdu, a0, wiu, wgu, k_k, k_a, seq, tm=256):
    t = p.shape[0]
    tm = min(tm, seq)
    w = MIX_WIDTH
    hb = SUBLANES
    kern = functools.partial(_rwkv_pre_kernel, tm=tm, seq=seq)
    row = lambda i: (i, 0)
    halo = lambda i: jnp.maximum(i * (tm // hb) - 1, 0)
    const = lambda i: (0, 0)
    vec = pl.BlockSpec((1, w), const)
    out = pl.BlockSpec((tm, w), row)
    return pl.pallas_call(
        kern,
        grid=(t // tm,),
        in_specs=[
            pl.BlockSpec((tm, 3 * w), lambda i: (i, 0)),
            pl.BlockSpec((tm, RW_LR_WIDTH), lambda i: (i, RW_LR // RW_LR_WIDTH)),
            pl.BlockSpec((hb, 3 * w), lambda i: (halo(i), 0)),
            pl.BlockSpec((hb, RW_LR_WIDTH), lambda i: (halo(i), RW_LR // RW_LR_WIDTH)),
            pl.BlockSpec((1, 3 * w), const),
            pl.BlockSpec((1, RW_LR_WIDTH), const),
            vec,
            pl.BlockSpec((LANES, w), const),
            vec,
            pl.BlockSpec((LANES, w), const),
            pl.BlockSpec((2 * LANES, w), const),
            vec, vec,
        ],
        out_specs=[out] * 7,
        out_shape=[jax.ShapeDtypeStruct((t, w), F32)] * 7,
        name="rwkv_pre",
        compiler_params=_cparams(("parallel",)),
    )(p, p, p, p, mu_rkv, mu_lr, w0, wdu, a0, wiu, wgu, k_k, k_a)


def _expand(x, lane_a):
    return jnp.concatenate([jnp.where(lane_a, x, 0.0), jnp.where(lane_a, 0.0, x)], axis=0)


def _dot3_stacked(lhs, rhs):
    m = lhs.shape[0]
    lh, ll = _split2(lhs)
    rh, rl = _split2(rhs)
    top = _dot(jnp.concatenate([lh, ll], axis=0), rh)
    return top[:m] + top[m:] + _dot(lh, rl)


def _unit_lower_inverses_wide(ls, lane_a):
    c = ls[0].shape[0]
    rows = lax.broadcasted_iota(jnp.int32, ls[0].shape, 0)
    cols = lax.broadcasted_iota(jnp.int32, ls[0].shape, 1) % c
    eye = jnp.where(rows == cols, 1.0, 0.0)
    ex = lambda x: _expand(x, lane_a)
    cat = lambda x, y: jnp.concatenate([x, y], axis=0)
    b = INVERSE_BASE_BLOCK
    same = (rows // b) == (cols // b)
    xs = [jnp.where(same, -l, 0.0) for l in ls]
    ts = [eye + x for x in xs]
    ps = [_dot3_stacked(x, ex(x)) for x in xs]
    prods = [_dot3_stacked(cat(t, p), ex(p)) for t, p in zip(ts, ps)]
    ts = [t + pr[:c] for t, pr in zip(ts, prods)]
    ds = [t + _dot3_stacked(t, ex(pr[c:])) for t, pr in zip(ts, prods)]
    while b < c:
        sub = ((rows // (2 * b)) == (cols // (2 * b))) & ((rows // b) != (cols // b))
        ys = [_dot3_stacked(jnp.where(sub, l, 0.0), ex(d)) for l, d in zip(ls, ds)]
        ds = [d - _dot3_stacked(d, ex(y)) for d, y in zip(ds, ys)]
        b *= 2
    return ds


def _rwkv_scan_kernel(r_ref, k_ref, v_ref, a_ref, b_ref, lw_ref, g_ref,
                      rk_ref, gng_ref, gnb_ref, o_ref, state_ref, *, chunk, npairs):
    c = chunk
    hd = RWKV_HEAD_DIM

    @pl.when(pl.program_id(2) == 0)
    def _():
        state_ref[...] = jnp.zeros_like(state_ref)

    ri = lax.broadcasted_iota(jnp.int32, (c, c), 0)
    ci = lax.broadcasted_iota(jnp.int32, (c, c), 1)
    tri = (ri >= ci).astype(BF16)
    rows = lax.broadcasted_iota(jnp.int32, (c, LANES), 0)
    lanes = lax.broadcasted_iota(jnp.int32, (c, LANES), 1)
    lane_a = lanes < hd
    strict = (lanes % hd) < rows
    incl = (lanes % hd) <= rows
    bi = lax.broadcasted_iota(jnp.int32, (LANES, LANES), 0) // hd
    bj = lax.broadcasted_iota(jnp.int32, (LANES, LANES), 1) // hd
    blockdiag = bi == bj
    ones64 = blockdiag.astype(BF16)

    pairs = range(npairs)
    sls = [slice(p * LANES, (p + 1) * LANES) for p in pairs]
    cat = lambda x, y: jnp.concatenate([x, y], axis=0)
    xp = lambda x: _expand(x, lane_a).astype(BF16)
    r = [r_ref[0, :, sl] for sl in sls]
    k = [k_ref[0, :, sl] for sl in sls]
    v = [v_ref[0, :, sl] for sl in sls]
    a = [a_ref[0, :, sl] for sl in sls]
    b = [b_ref[0, :, sl] for sl in sls]
    lw = [lw_ref[0, :, sl] for sl in sls]
    cs_all = _dot_const_lhs(tri, lw_ref[0])
    cs = [cs_all[:, sl] for sl in sls]
    c_last = [x[c - 1:c, :] for x in cs]
    e_neg = [jnp.exp(-x) for x in cs]
    e_end = [jnp.exp(cl - x) for cl, x in zip(c_last, cs)]
    ar = [cat(a[p] * jnp.exp(cs[p] - lw[p]), r[p] * jnp.exp(cs[p])).astype(BF16) for p in pairs]
    s_k = [_dot_nt(ar[p], xp(k[p] * e_neg[p])) for p in pairs]
    s_b = [_dot_nt(ar[p], xp(b[p] * e_neg[p])) for p in pairs]
    tinv = _unit_lower_inverses_wide([jnp.where(strict, s[:c], 0.0) for s in s_b], lane_a)
    lm_k = [cat(jnp.where(strict, s[:c], 0.0), jnp.where(incl, s[c:], 0.0)).astype(BF16)
            for s in s_k]
    m_rb = [jnp.where(incl, s[c:], 0.0).astype(BF16) for s in s_b]
    state = [state_ref[p] for p in pairs]
    ah = [_dot_nt(ar[p], state[p].astype(BF16)) for p in pairs]
    lv = [_dot(lm_k[p], xp(v[p])) for p in pairs]
    u = [_dot3_stacked(tinv[p], _expand(ah[p][:c] + lv[p][:c], lane_a)) for p in pairs]
    y = [ah[p][c:] + lv[p][c:] - _dot(m_rb[p], xp(u[p])) for p in pairs]
    upd = [_dot_tn(cat(v[p], -u[p]).astype(BF16),
                   cat(k[p] * e_end[p], b[p] * e_end[p]).astype(BF16)) for p in pairs]
    for p in pairs:
        state_ref[p] = jnp.where(blockdiag, state[p] * jnp.exp(c_last[p]) + upd[p], 0.0)

    sums = _group_sums(y + [r[p] * k[p] * rk_ref[0:1, sls[p]] for p in pairs], ones64)
    mean = [x * (1.0 / hd) for x in sums[:npairs]]
    bonus = [x * v[p] for p, x in zip(pairs, sums[npairs:])]
    yc = [x - m for x, m in zip(y, mean)]
    var = [x * (1.0 / hd) for x in _group_sums([x * x for x in yc], ones64)]
    for p in pairs:
        sl = sls[p]
        yn = yc[p] * lax.rsqrt(var[p] + RWKV_GN_EPS) * gng_ref[0:1, sl] + gnb_ref[0:1, sl]
        o_ref[0, :, sl] = ((yn + bonus[p]) * g_ref[0, :, sl]).astype(o_ref.dtype)


def rwkv_scan(r, k, v, a, b, lw, g, r_k, gn_g, gn_b, batch, seq, lane_block=1536):
    w = MIX_WIDTH
    c = RWKV_CHUNK
    lane_block = min(lane_block, w)
    npairs = lane_block // LANES
    ngroups = w // lane_block
    shp = (batch, seq, w)
    args = [z.reshape(shp) for z in (r, k, v, a, b, lw, g)]
    tok = pl.BlockSpec((1, c, lane_block), lambda bi, gi, n: (bi, n, gi))
    vec = pl.BlockSpec((1, lane_block), lambda bi, gi, n: (0, gi))
    kern = functools.partial(_rwkv_scan_kernel, chunk=c, npairs=npairs)
    return pl.pallas_call(
        kern,
        grid=(batch, ngroups, seq // c),
        in_specs=[tok] * 7 + [vec] * 3,
        out_specs=tok,
        out_shape=jax.ShapeDtypeStruct(shp, BF16),
        scratch_shapes=[pltpu.VMEM((npairs, LANES, LANES), F32)],
        name="rwkv_scan",
        compiler_params=_cparams(("parallel", "parallel", "arbitrary")),
    )(*args, r_k, gn_g, gn_b)


GD_QKV = 0
GD_QKV_WIDTH = 2 * GDN_QK_WIDTH + MIX_WIDTH
GD_Z = GD_QKV_WIDTH
GD_QMEM = GD_Z + MIX_WIDTH
GD_BA = GD_QMEM + MEM_WIDTH
GD_N = 5376


def _gdn_pre_kernel(x_ref, ba_ref, xh_ref, cw_ref, alog_ref, dtb_ref,
                    q_o, k_o, v_o, gb_o, *, tm, seq):
    i = pl.program_id(0)
    seq_start = (i * tm) % seq == 0
    x = x_ref[...]
    halo = xh_ref[...]
    cw = cw_ref[...]
    y = cw[GDN_CONV - 1:GDN_CONV] * x
    for s in range(1, GDN_CONV):
        y = y + cw[GDN_CONV - 1 - s:GDN_CONV - s] * _shift_rows(x, halo, s, seq_start)
    y = _silu(y)
    ones128 = jnp.ones((LANES, LANES), BF16)
    qw = GDN_QK_WIDTH
    q = y[:, :qw]
    k = y[:, qw:2 * qw]
    qq, kk = _group_sums([q * q, k * k], ones128)
    q = q * lax.rsqrt(qq + 1e-6) * (GDN_HEAD_DIM ** -0.5)
    k = k * lax.rsqrt(kk + 1e-6)
    q_o[...] = q
    k_o[...] = k
    v_o[...] = y[:, 2 * qw:]
    ba = ba_ref[...]
    lanes = lax.broadcasted_iota(jnp.int32, ba.shape, 1)
    beta = _sigmoid(ba)
    g = -jnp.exp(alog_ref[...]) * _softplus(ba + dtb_ref[...])
    gb_o[...] = jnp.where(lanes < GDN_V_HEADS, beta, g)


def gdn_pre(p, conv_w, alog_row, dtb_row, seq, tm=256):
    t = p.shape[0]
    tm = min(tm, seq)
    hb = SUBLANES
    kern = functools.partial(_gdn_pre_kernel, tm=tm, seq=seq)
    row = lambda i: (i, 0)
    const = lambda i: (0, 0)
    halo = lambda i: jnp.maximum(i * (tm // hb) - 1, 0)
    return pl.pallas_call(
        kern,
        grid=(t // tm,),
        in_specs=[
            pl.BlockSpec((tm, GD_QKV_WIDTH), row),
            pl.BlockSpec((tm, LANES), lambda i: (i, GD_BA // LANES)),
            pl.BlockSpec((hb, GD_QKV_WIDTH), lambda i: (halo(i), 0)),
            pl.BlockSpec((GDN_CONV, GD_QKV_WIDTH), const),
            pl.BlockSpec((1, LANES), const),
            pl.BlockSpec((1, LANES), const),
        ],
        out_specs=[pl.BlockSpec((tm, GDN_QK_WIDTH), row),
                   pl.BlockSpec((tm, GDN_QK_WIDTH), row),
                   pl.BlockSpec((tm, MIX_WIDTH), row),
                   pl.BlockSpec((tm, LANES), row)],
        out_shape=[jax.ShapeDtypeStruct((t, GDN_QK_WIDTH), F32),
                   jax.ShapeDtypeStruct((t, GDN_QK_WIDTH), F32),
                   jax.ShapeDtypeStruct((t, MIX_WIDTH), F32),
                   jax.ShapeDtypeStruct((t, LANES), F32)],
        name="gdn_pre",
        compiler_params=_cparams(("parallel",)),
    )(p, p, p, conv_w, alog_row, dtb_row)


def _gdn_scan_kernel(q_ref, k_ref, v_ref, gb_ref, z_ref, ng_ref, o_ref, state_ref, *, chunk):
    c = chunk
    dh = GDN_HEAD_DIM
    nh = GDN_V_HEADS
    npairs = nh // 2

    @pl.when(pl.program_id(1) == 0)
    def _():
        state_ref[...] = jnp.zeros_like(state_ref)

    ri = lax.broadcasted_iota(jnp.int32, (c, c), 0)
    ci = lax.broadcasted_iota(jnp.int32, (c, c), 1)
    tri = (ri >= ci).astype(BF16)
    ones_cc = jnp.ones((c, c), BF16)
    rows = lax.broadcasted_iota(jnp.int32, (c, LANES), 0)
    lanes = lax.broadcasted_iota(jnp.int32, (c, LANES), 1)
    lane_a = lanes < c
    col = lanes % c
    causal = col <= rows
    strict = col < rows
    diag = col == rows
    zeros = jnp.zeros((c, dh), F32)
    cat = lambda x, y: jnp.concatenate([x, y], axis=0)
    wide = lambda xs, i: jnp.where(lane_a, xs[2 * i], xs[2 * i + 1])

    gb = gb_ref[0]
    gcum = _dot_const_lhs(tri, gb)
    bcol = lambda x, j: jnp.broadcast_to(x[:, j:j + 1], (c, dh))
    beta = [bcol(gb, h) for h in range(nh)]
    gc = [bcol(gcum, nh + h) for h in range(nh)]
    e_gc = [jnp.exp(x) for x in gc]
    g_last = [x[c - 1:c, :] for x in gc]
    q = [q_ref[0, :, i * dh:(i + 1) * dh] for i in range(npairs)]
    k = [k_ref[0, :, i * dh:(i + 1) * dh] for i in range(npairs)]
    v = [v_ref[0, :, h * dh:(h + 1) * dh] for h in range(nh)]

    gcs_w = [wide(gc, i) for i in range(npairs)]
    gcr_all = _dot_const_lhs(ones_cc, jnp.concatenate([jnp.where(diag, x, 0.0) for x in gcs_w], axis=1))
    gcr_w = [gcr_all[:, i * LANES:(i + 1) * LANES] for i in range(npairs)]
    decay_w = [jnp.where(causal, jnp.exp(jnp.minimum(s - r, 0.0)), 0.0)
               for s, r in zip(gcs_w, gcr_w)]
    sc = [_dot_nt(cat(k[i], q[i]).astype(BF16), cat(k[i], k[i]).astype(BF16))
          for i in range(npairs)]
    lmat_w = [jnp.where(strict, sc[i][:c] * wide(beta, i) * decay_w[i], 0.0) for i in range(npairs)]
    tinv = _unit_lower_inverses_wide(lmat_w, lane_a)
    a_qk_w = [(sc[i][c:] * decay_w[i]).astype(BF16) for i in range(npairs)]

    def pair_rhs(i):
        ha, hb = 2 * i, 2 * i + 1
        top = jnp.concatenate([v[ha] * beta[ha], zeros, k[i] * (beta[ha] * e_gc[ha]), zeros], axis=1)
        bot = jnp.concatenate([zeros, v[hb] * beta[hb], zeros, k[i] * (beta[hb] * e_gc[hb])], axis=1)
        return cat(top, bot)

    uw = [_dot3_stacked(tinv[i], pair_rhs(i)) for i in range(npairs)]
    u = [uw[h // 2][:, (h % 2) * dh:(h % 2 + 1) * dh] for h in range(nh)]
    w = [uw[h // 2][:, (2 + h % 2) * dh:(3 + h % 2) * dh] for h in range(nh)]
    state = [state_ref[h] for h in range(nh)]
    ws = [_dot(cat(w[h], q[h // 2] * e_gc[h]).astype(BF16), state[h].astype(BF16))
          for h in range(nh)]
    v_new = [u[h] - ws[h][:c] for h in range(nh)]
    vn_bf = [x.astype(BF16) for x in v_new]
    zb = jnp.zeros((c, dh), BF16)
    intra = [_dot(a_qk_w[i],
                  cat(jnp.concatenate([vn_bf[2 * i], zb], axis=1),
                      jnp.concatenate([zb, vn_bf[2 * i + 1]], axis=1)))
             for i in range(npairs)]
    out = [ws[h][c:] + intra[h // 2][:, (h % 2) * dh:(h % 2 + 1) * dh] for h in range(nh)]
    upd = [_dot_tn((k[h // 2] * jnp.exp(g_last[h] - gc[h])).astype(BF16), vn_bf[h])
           for h in range(nh)]
    for h in range(nh):
        state_ref[h] = state[h] * jnp.exp(g_last[h]) + upd[h]
    for h in range(nh):
        ms = jnp.mean(out[h] * out[h], axis=-1, keepdims=True)
        o = out[h] * lax.rsqrt(ms + NORM_EPS) * ng_ref[...]
        o = o * _silu(z_ref[0, :, h * dh:(h + 1) * dh])
        o_ref[0, :, h * dh:(h + 1) * dh] = o.astype(o_ref.dtype)


def gdn_scan(q, k, v, gb, p3, norm_g, batch, seq):
    c = GDN_CHUNK
    qw = GDN_QK_WIDTH
    w = MIX_WIDTH
    kern = functools.partial(_gdn_scan_kernel, chunk=c)
    tokq = pl.BlockSpec((1, c, qw), lambda bi, n: (bi, n, 0))
    tokv = pl.BlockSpec((1, c, w), lambda bi, n: (bi, n, 0))
    return pl.pallas_call(
        kern,
        grid=(batch, seq // c),
        in_specs=[tokq, tokq, tokv,
                  pl.BlockSpec((1, c, LANES), lambda bi, n: (bi, n, 0)),
                  pl.BlockSpec((1, c, w), lambda bi, n: (bi, n, GD_Z // w)),
                  pl.BlockSpec((1, GDN_HEAD_DIM), lambda bi, n: (0, 0))],
        out_specs=tokv,
        out_shape=jax.ShapeDtypeStruct((batch, seq, w), BF16),
        scratch_shapes=[pltpu.VMEM((GDN_V_HEADS, GDN_HEAD_DIM, GDN_HEAD_DIM), F32)],
        name="gdn_scan",
        compiler_params=_cparams(("parallel", "arbitrary")),
    )(q.reshape(batch, seq, qw), k.reshape(batch, seq, qw), v.reshape(batch, seq, w),
      gb.reshape(batch, seq, LANES), p3, norm_g.reshape(1, GDN_HEAD_DIM))


def _pad_cols(w, n):
    return jnp.pad(w, ((0, 0), (0, n - w.shape[1])))


def _pad_rows(w, n):
    return jnp.pad(w, ((0, n - w.shape[0]), (0, 0)))


def _mixer_a(x, g, w_in_all, layer, sinks, batch, seq):
    p = norm_matmul(x, g, w_in_all, tm=1024, tn=512, layer=layer, out_dtype=BF16)
    p3 = p.reshape(batch, seq, -1)
    y = swa_attention(p3, sinks, q_col=0, k_col=MIX_WIDTH,
                      v_col=MIX_WIDTH + SWA_KV_HEADS * SWA_HEAD_DIM)
    return y, p3, 2048


def _mixer_b(x, g, w_in, mu, w0, w_decay_up, a0, w_iclr_up, w_gate_up, k_k, k_a, r_k, gn_g, gn_b,
             batch, seq):
    w = MIX_WIDTH
    dr = w_decay_up.shape[0]
    ir = w_iclr_up.shape[0]
    gr = w_gate_up.shape[0]
    o_wd, o_ad, o_gd, o_qm = 3 * w, 3 * w + dr, 3 * w + dr + ir, 3 * w + dr + ir + gr
    w_in_r = jnp.concatenate([
        w_in[:, :3 * w], w_in[:, o_qm:],
        _pad_cols(w_in[:, o_wd:o_ad], LANES), _pad_cols(w_in[:, o_ad:o_gd], LANES),
        w_in[:, o_gd:o_qm]], axis=1).astype(BF16)
    mu_rkv = mu[:3 * w].reshape(1, -1)
    mu_lr = jnp.concatenate([jnp.pad(mu[o_wd:o_ad], (0, LANES - dr)),
                             jnp.pad(mu[o_ad:o_gd], (0, LANES - ir)),
                             mu[o_gd:o_qm]]).reshape(1, -1)
    p = norm_matmul(x, g, w_in_r, tm=1024, tn=512)
    row = lambda z: z.reshape(1, -1)
    r, k, v, a, b, lw, gate = rwkv_pre(
        p, mu_rkv, mu_lr, row(w0), _pad_rows(w_decay_up, LANES).astype(BF16), row(a0),
        _pad_rows(w_iclr_up, LANES).astype(BF16), w_gate_up.astype(BF16), row(k_k), row(k_a), seq)
    y = rwkv_scan(r, k, v, a, b, lw, gate, row(r_k), row(gn_g), row(gn_b), batch, seq)
    return y, p.reshape(batch, seq, -1), RW_QMEM


def _mixer_c(x, g, w_in, conv_w, a_log, dt_bias, norm_g, batch, seq):
    w = MIX_WIDTH
    nh = GDN_V_HEADS
    o_z = GD_QKV_WIDTH
    o_bt = o_z + w
    o_at = o_bt + nh
    o_qm = o_at + nh
    ba = w_in[:, o_bt:o_qm]
    w_in_r = jnp.concatenate([
        w_in[:, :o_z], w_in[:, o_z:o_bt], w_in[:, o_qm:], _pad_cols(ba, LANES)], axis=1)
    w_in_r = _pad_cols(w_in_r, GD_N).astype(BF16)
    p = norm_matmul(x, g, w_in_r, tm=1024, tn=768)
    pad = LANES - 2 * nh
    alog_row = jnp.concatenate([jnp.zeros((nh,), F32), a_log, jnp.zeros((pad,), F32)]).reshape(1, -1)
    dtb_row = jnp.concatenate([jnp.zeros((nh,), F32), dt_bias, jnp.zeros((pad,), F32)]).reshape(1, -1)
    q, k, v, gb = gdn_pre(p, conv_w, alog_row, dtb_row, seq)
    p3 = p.reshape(batch, seq, -1)
    y = gdn_scan(q, k, v, gb, p3, norm_g, batch, seq)
    return y, p3, GD_QMEM


def kernel(x, mem, attn_norm, mem_norm, w_mem_kv, w_out, ffn_norm, w_ffn_up, ffn_conv, w_ffn_down,
           final_norm, a_w_in, a_sinks, b_w_in, b_mu, b_w0, b_w_decay_up, b_a0, b_w_iclr_up,
           b_w_gate_up, b_k_k, b_k_a, b_r_k, b_gn_g, b_gn_b, c_w_in, c_conv, c_a_log, c_dt_bias,
           c_norm_g):
    batch, seq, d = x.shape
    mlen = mem.shape[1]
    depth = attn_norm.shape[0]
    x = x.reshape(batch * seq, d)
    mem2 = mem.reshape(batch * mlen, d)
    w_mem_kv_b = w_mem_kv.astype(BF16)
    w_out_b = w_out.astype(BF16)
    w_up_b = w_ffn_up.astype(BF16)
    w_down_b = w_ffn_down.astype(BF16)
    a_w_in_b = a_w_in.astype(BF16)
    for i in range(depth):
        kind, j = i % 3, i // 3
        g = attn_norm[i]
        mem_kv = norm_matmul(mem2, mem_norm[i], w_mem_kv_b, tm=1024, tn=512, layer=i,
                             out_dtype=BF16)
        mem_kv3 = mem_kv.reshape(batch, mlen, -1)
        if kind == 0:
            y, p3, q_col = _mixer_a(x, g, a_w_in_b, j, a_sinks[j], batch, seq)
        elif kind == 1:
            y, p3, q_col = _mixer_b(x, g, b_w_in[j], b_mu[j], b_w0[j], b_w_decay_up[j], b_a0[j],
                                    b_w_iclr_up[j], b_w_gate_up[j], b_k_k[j], b_k_a[j],
                                    b_r_k[j].reshape(-1), b_gn_g[j], b_gn_b[j], batch, seq)
        else:
            y, p3, q_col = _mixer_c(x, g, c_w_in[j], c_conv[j], c_a_log[j], c_dt_bias[j],
                                    c_norm_g[j], batch, seq)
        y_mem = mem_attention(p3, q_col, mem_kv3)
        x = outproj(y.reshape(batch * seq, -1), y_mem.reshape(batch * seq, -1), w_out_b, i, x)
        x = ffn(x, ffn_norm[i], w_up_b, ffn_conv[i], w_down_b, i, seq, final_norm,
                final_norm=(i == depth - 1))
    return x.reshape(batch, seq, d)
```
